```python
import math
import jax
import jax.numpy as jnp
from jax import lax
import numpy as np

D_MODEL = 1024
BATCH = 8
SEQ = 4096
DEPTH = 4

GRID_W = 64
CTX_LEN = 256
ROPE_BASE = 10000.0
NORM_EPS = 1e-6
BLOCK = 128
NEG_INF = -1e30

MLA_HEADS = 6
MLA_Q_RANK = 256
MLA_KV_RANK = 128
MLA_NOPE = 64
MLA_ROPE = 32
MLA_V = 64
SSM_HEADS = 6
SSM_HEAD_DIM = 64
SSM_D_INNER = SSM_HEADS * SSM_HEAD_DIM
SSM_GROUPS = 2
SSM_STATE = 128
SSM_CONV = 3
SSM_CHUNK = 128
SSM_CONV_CH = SSM_D_INNER + 2 * SSM_GROUPS * SSM_STATE
SWA_HEADS = 4
SWA_KV_HEADS = 2
SWA_HEAD_DIM = 64
WINDOW = 128
FFN_HIDDEN = 2816
FFN_CONV = 3

MIX_WIDTH = MLA_HEADS * MLA_V + SSM_D_INNER + SWA_HEADS * SWA_HEAD_DIM
IN_SPLITS = (MLA_Q_RANK, MLA_KV_RANK, MLA_ROPE,
             SSM_D_INNER, SSM_CONV_CH, 2 * SSM_HEADS,
             SWA_HEADS * SWA_HEAD_DIM, SWA_KV_HEADS * SWA_HEAD_DIM, SWA_KV_HEADS * SWA_HEAD_DIM)
IN_WIDTH = sum(IN_SPLITS)

kernel_name = 'hybrid_mla_ssd_swa_diffusion_block'


def rms_norm(x, g):
    xf = x.astype(jnp.float32)
    y = xf * lax.rsqrt(jnp.mean(xf * xf, axis=-1, keepdims=True) + NORM_EPS)
    return (y * g.astype(jnp.float32)).astype(x.dtype)


def modulate(h, shift, scale):
    return h * (1 + scale) + shift


def split_cols(p, sizes):
    idx = np.cumsum(np.array(sizes))[:-1].tolist()
    return jnp.split(p, idx, axis=-1)


def dwconv(x, w, b):
    k, ch = w.shape
    y = lax.conv_general_dilated(x, w[:, None, :].astype(x.dtype), window_strides=(1,),
                                 padding=[((k - 1) // 2, k // 2)],
                                 dimension_numbers=('NWC', 'WIO', 'NWC'), feature_group_count=ch)
    return y + b.astype(x.dtype)


def rope_1d(x, pos):
    n = x.shape[-1] // 2
    inv = jnp.power(ROPE_BASE, -jnp.arange(n, dtype=jnp.float32) / n)
    ang = pos.astype(jnp.float32)[:, None] * inv
    cos, sin = jnp.cos(ang)[:, None, :], jnp.sin(ang)[:, None, :]
    xf = x.astype(jnp.float32)
    x1, x2 = xf[..., :n], xf[..., n:]
    return jnp.concatenate([x1 * cos - x2 * sin, x1 * sin + x2 * cos], axis=-1).astype(x.dtype)


def axial_rope(x, row, col):
    h = x.shape[-1] // 2
    return jnp.concatenate([rope_1d(x[..., :h], row), rope_1d(x[..., h:], col)], axis=-1)


def joint_softmax(logits):
    sizes = [l.shape[-1] for l in logits]
    p = jax.nn.softmax(jnp.concatenate([l.astype(jnp.float32) for l in logits], axis=-1), axis=-1)
    return split_cols(p, sizes)


def mla_queries(qa, g, w_uq, pos):
    b, l, _ = qa.shape
    q = (rms_norm(qa, g) @ w_uq).reshape(b, l, MLA_HEADS, MLA_NOPE + MLA_ROPE)
    if pos is not None:
        q = jnp.concatenate([q[..., :MLA_NOPE], axial_rope(q[..., MLA_NOPE:], *pos)], axis=-1)
    return q * (MLA_NOPE + MLA_ROPE) ** -0.5


def mla_keys_values(kva, kr, g, w_ukv, pos):
    b, l, _ = kva.shape
    kv = (rms_norm(kva, g) @ w_ukv).reshape(b, l, MLA_HEADS, MLA_NOPE + MLA_V)
    k_pe = kr[:, :, None, :]
    if pos is not None:
        k_pe = axial_rope(k_pe, *pos)
    k = jnp.concatenate([kv[..., :MLA_NOPE], jnp.broadcast_to(k_pe, (b, l, MLA_HEADS, MLA_ROPE))], axis=-1)
    return k, kv[..., MLA_NOPE:]


def mla_latent_attention(q, k, v, kc, vc):
    b, s, h, dq = q.shape
    nb = s // BLOCK
    qb = jnp.moveaxis(q.reshape(b, nb, BLOCK, h, dq), 1, 0)

    def one_block(qblk):
        lc = jnp.einsum('bqhd,bkhd->bhqk', qblk, kc)
        ll = jnp.einsum('bqhd,bkhd->bhqk', qblk, k)
        pc, pl = joint_softmax([lc, ll])
        return (jnp.einsum('bhqk,bkhd->bqhd', pc.astype(v.dtype), vc)
                + jnp.einsum('bhqk,bkhd->bqhd', pl.astype(v.dtype), v))

    out = lax.map(one_block, qb)
    return jnp.moveaxis(out, 0, 1).reshape(b, s, h * MLA_V)


def mla_context_attention(qc, kc, vc):
    b, n, h, _ = qc.shape
    p = jax.nn.softmax(jnp.einsum('bqhd,bkhd->bhqk', qc, kc).astype(jnp.float32), axis=-1)
    return jnp.einsum('bhqk,bkhd->bqhd', p.astype(vc.dtype), vc).reshape(b, n, h * MLA_V)


def ssd_inputs(xbc, dt_raw, conv_w, conv_b, dt_bias):
    b, l, _ = xbc.shape
    xbc = jax.nn.silu(dwconv(xbc, conv_w, conv_b))
    xs, bm, cm = split_cols(xbc, (SSM_D_INNER, SSM_GROUPS * SSM_STATE, SSM_GROUPS * SSM_STATE))
    dt = jax.nn.softplus(dt_raw.astype(jnp.float32).reshape(b, l, 2, SSM_HEADS)
                         + dt_bias.astype(jnp.float32))
    return (xs.reshape(b, l, SSM_HEADS, SSM_HEAD_DIM),
            bm.reshape(b, l, SSM_GROUPS, SSM_STATE),
            cm.reshape(b, l, SSM_GROUPS, SSM_STATE), dt)


def ssd_chunked_scan(x, dt, a, bm, cm, h0, need_y):
    b, l, h, p = x.shape
    g, n = bm.shape[-2:]
    r, q = h // g, SSM_CHUNK
    nc = l // q
    f32 = jnp.float32
    xdt = (x.astype(f32) * dt[..., None]).reshape(b, nc, q, g, r, p)
    acs = jnp.cumsum((dt * a).reshape(b, nc, q, g, r), axis=2)
    bc = bm.astype(f32).reshape(b, nc, q, g, n)
    cc = cm.astype(f32).reshape(b, nc, q, g, n)
    a_last = acs[:, :, -1]
    states = jnp.einsum('bcqgn,bcqgr,bcqgrp->bcgrpn', bc, jnp.exp(a_last[:, :, None] - acs), xdt)

    def carry(hs, inp):
        st, al = inp
        return hs * jnp.exp(al)[..., None, None] + st, hs

    h_final, h_start = lax.scan(carry, h0, (jnp.moveaxis(states, 1, 0), jnp.moveaxis(a_last, 1, 0)))
    if not need_y:
        return None, h_final
    h_start = jnp.moveaxis(h_start, 0, 1)
    y_off = jnp.einsum('bcqgn,bcgrpn,bcqgr->bcqgrp', cc, h_start, jnp.exp(acs))
    acs_t = jnp.moveaxis(acs, 2, -1)
    diff = acs_t[..., :, None] - acs_t[..., None, :]
    tril = jnp.tril(jnp.ones((q, q), dtype=bool))
    decay = jnp.where(tril, jnp.exp(jnp.where(tril, diff, 0.0)), 0.0)
    cb = jnp.einsum('bcqgn,bckgn->bcgqk', cc, bc)
    y_diag = jnp.einsum('bcgqk,bcgrqk,bckgrp->bcqgrp', cb, decay, xdt)
    return (y_diag + y_off).reshape(b, l, h, p).astype(x.dtype), h_final


def ssd_bidirectional(xs, bm, cm, dt, a, h0, need_y):
    ys, finals = [], []
    for d in range(2):
        f = (lambda t: jnp.flip(t, axis=1)) if d == 1 else (lambda t: t)
        y, h_t = ssd_chunked_scan(f(xs), f(dt[:, :, d]), a[d], f(bm), f(cm), h0[d], need_y)
        ys.append(f(y) if need_y else None)
        finals.append(h_t)
    y = ys[0] + ys[1] if need_y else None
    return y, (finals[0], finals[1])


def ssd_output(y, xs, z, d_skip, norm_g):
    b, l = z.shape[:2]
    y = (y + d_skip[:, None].astype(y.dtype) * xs).reshape(b, l, SSM_D_INNER)
    return rms_norm(y * jax.nn.silu(z), norm_g)


def swa_latent_attention(q, k, v, kc, vc, sink):
    b, s, hq, d = q.shape
    hk = k.shape[2]
    r, nb = hq // hk, s // BLOCK
    qb = q.reshape(b, nb, BLOCK, hk, r, d)

    def band(t):
        tp = jnp.pad(t.reshape(b, nb, BLOCK, hk, d), ((0, 0), (1, 1), (0, 0), (0, 0), (0, 0)))
        return jnp.concatenate([tp[:, :-2], tp[:, 1:-1], tp[:, 2:]], axis=2)

    kb, vb = band(k), band(v)
    blk = jnp.arange(nb)[:, None, None]
    qpos = blk * BLOCK + jnp.arange(BLOCK)[None, :, None]
    kpos = (blk - 1) * BLOCK + jnp.arange(3 * BLOCK)[None, None, :]
    valid = (jnp.abs(kpos - qpos) <= WINDOW) & (kpos >= 0) & (kpos < s)
    lb = jnp.einsum('bnqgrd,bnkgd->bngrqk', qb, kb).astype(jnp.float32)
    lb = jnp.where(valid[None, :, None, None], lb, NEG_INF)
    lc = jnp.einsum('bnqgrd,bkgd->bngrqk', qb, kc)
    ls = jnp.broadcast_to(sink.reshape(hk, r)[None, None, :, :, None, None].astype(jnp.float32),
                          lc.shape[:-1] + (1,))
    _, pc, pb = joint_softmax([ls, lc, lb])
    out = (jnp.einsum('bngrqk,bkgd->bnqgrd', pc.astype(v.dtype), vc)
           + jnp.einsum('bngrqk,bnkgd->bnqgrd', pb.astype(v.dtype), vb))
    return out.reshape(b, s, hq * d)


def swa_context_attention(qc, kc, vc, sink):
    b, n, hq, d = qc.shape
    hk = kc.shape[2]
    r = hq // hk
    lc = jnp.einsum('bqgrd,bkgd->bgrqk', qc.reshape(b, n, hk, r, d), kc)
    ls = jnp.broadcast_to(sink.reshape(hk, r)[None, :, :, None, None].astype(jnp.float32),
                          lc.shape[:-1] + (1,))
    _, pc = joint_softmax([ls, lc])
    return jnp.einsum('bgrqk,bkgd->bqgrd', pc.astype(vc.dtype), vc).reshape(b, n, hq * d)


def conv_glu(h, w_up, conv_w, conv_b, w_down):
    val, gate = jnp.split(h @ w_up, 2, axis=-1)
    return (jax.nn.silu(dwconv(gate, conv_w, conv_b)) * val) @ w_down


def hybrid_layer(x, xc, mod, modc, p, pos, update_ctx):
    b, s, _ = x.shape
    sh1, sc1, g1, sh2, sc2, g2 = [m[:, None, :] for m in jnp.split(mod, 6, axis=-1)]
    shc1, scc1, gc1, shc2, scc2, gc2 = jnp.split(modc, 6, axis=-1)

    h = modulate(rms_norm(x, p['norm1_g']), sh1, sc1)
    hc = modulate(rms_norm(xc, p['norm1_g']), shc1, scc1)
    qa, kva, kr, z, xbc, dtr, swq, swk, swv = split_cols(h @ p['w_in'], IN_SPLITS)
    qac, kvac, krc, zc, xbcc, dtrc, swqc, swkc, swvc = split_cols(hc @ p['w_in'], IN_SPLITS)

    ka_c, va_c = mla_keys_values(kvac, krc, p['mla_kv_norm_g'], p['mla_w_ukv'], None)
    ka, va = mla_keys_values(kva, kr, p['mla_kv_norm_g'], p['mla_w_ukv'], pos)
    qa_l = mla_queries(qa, p['mla_q_norm_g'], p['mla_w_uq'], pos)
    y_a = mla_latent_attention(qa_l, ka, va, ka_c, va_c)

    a = -jnp.exp(p['ssm_a_log'].astype(jnp.float32))
    xs_c, bm_c, cm_c, dt_c = ssd_inputs(xbcc, dtrc, p['ssm_conv_w'], p['ssm_conv_b'], p['ssm_dt_bias'])
    xs, bm, cm, dt = ssd_inputs(xbc, dtr, p['ssm_conv_w'], p['ssm_conv_b'], p['ssm_dt_bias'])
    h0 = jnp.zeros((b, SSM_GROUPS, SSM_HEADS // SSM_GROUPS, SSM_HEAD_DIM, SSM_STATE), jnp.float32)
    yb_c, h_ctx = ssd_bidirectional(xs_c, bm_c, cm_c, dt_c, a, (h0, h0), update_ctx)
    yb, _ = ssd_bidirectional(xs, bm, cm, dt, a, h_ctx, True)
    y_b = ssd_output(yb, xs, z, p['ssm_d'], p['ssm_norm_g'])

    scale = SWA_HEAD_DIM ** -0.5
    n = xc.shape[1]
    qs = axial_rope(swq.reshape(b, s, SWA_HEADS, SWA_HEAD_DIM), *pos) * scale
    ks = axial_rope(swk.reshape(b, s, SWA_KV_HEADS, SWA_HEAD_DIM), *pos)
    vs = swv.reshape(b, s, SWA_KV_HEADS, SWA_HEAD_DIM)
    ks_c = swkc.reshape(b, n, SWA_KV_HEADS, SWA_HEAD_DIM)
    vs_c = swvc.reshape(b, n, SWA_KV_HEADS, SWA_HEAD_DIM)
    y_c = swa_latent_attention(qs, ks, vs, ks_c, vs_c, p['swa_sink'])

    x = x + g1 * (jnp.concatenate([y_a, y_b, y_c], axis=-1) @ p['w_out'])
    h2 = modulate(rms_norm(x, p['norm2_g']), sh2, sc2)
    x = x + g2 * conv_glu(h2, p['ffn_w_up'], p['ffn_conv_w'], p['ffn_conv_b'], p['ffn_w_down'])

    if update_ctx:
        ya_c = mla_context_attention(mla_queries(qac, p['mla_q_norm_g'], p['mla_w_uq'], None), ka_c, va_c)
        yb_c = ssd_output(yb_c, xs_c, zc, p['ssm_d'], p['ssm_norm_g'])
        yc_c = swa_context_attention(swqc.reshape(b, n, SWA_HEADS, SWA_HEAD_DIM) * scale, ks_c, vs_c, p['swa_sink'])
        xc = xc + gc1 * (jnp.concatenate([ya_c, yb_c, yc_c], axis=-1) @ p['w_out'])
        hc2 = modulate(rms_norm(xc, p['norm2_g']), shc2, scc2)
        xc = xc + gc2 * conv_glu(hc2, p['ffn_w_up'], p['ffn_conv_w'], p['ffn_conv_b'], p['ffn_w_down'])
    return x, xc


def setup_inputs(seed: int = 0) -> dict:
    key = jax.random.key(seed)
    keys = iter(jax.random.split(key, 40))
    d, nl, f = D_MODEL, DEPTH, FFN_HIDDEN

    def normal(shape, scale):
        return jax.random.normal(next(keys), shape, jnp.float32) * scale

    def gain(shape):
        return 1.0 + normal(shape, 0.02)

    dt0 = jnp.exp(jax.random.uniform(next(keys), (nl, 2, SSM_HEADS), jnp.float32,
                                     math.log(1e-3), math.log(1e-1)))
    return {
        'x': normal((BATCH, SEQ, d), 1.0),
        'c': normal((BATCH, d), 1.0),
        'ctx': normal((BATCH, CTX_LEN, d), 1.0),
        'c_ctx': normal((d,), 1.0),
        'w_mod': normal((nl, d, 6 * d), 0.5 * d ** -0.5),
        'b_mod': normal((nl, 6 * d), 0.02),
        'norm1_g': gain((nl, d)),
        'norm2_g': gain((nl, d)),
        'w_in': normal((nl, d, IN_WIDTH), d ** -0.5),
        'mla_q_norm_g': gain((nl, MLA_Q_RANK)),
        'mla_kv_norm_g': gain((nl, MLA_KV_RANK)),
        'mla_w_uq': normal((nl, MLA_Q_RANK, MLA_HEADS * (MLA_NOPE + MLA_ROPE)), MLA_Q_RANK ** -0.5),
        'mla_w_ukv': normal((nl, MLA_KV_RANK, MLA_HEADS * (MLA_NOPE + MLA_V)), MLA_KV_RANK ** -0.5),
        'ssm_conv_w': normal((nl, SSM_CONV, SSM_CONV_CH), SSM_CONV ** -0.5),
        'ssm_conv_b': normal((nl, SSM_CONV_CH), 0.02),
        'ssm_dt_bias': dt0 + jnp.log(-jnp.expm1(-dt0)),
        'ssm_a_log': jnp.log(jax.random.uniform(next(keys), (nl, 2, SSM_HEADS), jnp.float32, 1.0, 16.0)),
        'ssm_d': gain((nl, SSM_HEADS)),
        'ssm_norm_g': gain((nl, SSM_D_INNER)),
        'swa_sink': normal((nl, SWA_HEADS), 0.5),
        'w_out': normal((nl, MIX_WIDTH, d), MIX_WIDTH ** -0.5),
        'ffn_w_up': normal((nl, d, 2 * f), d ** -0.5),
        'ffn_conv_w': normal((nl, FFN_CONV, f), FFN_CONV ** -0.5),
        'ffn_conv_b': normal((nl, f), 0.02),
        'ffn_w_down': normal((nl, f, d), f ** -0.5),
        'final_norm_g': gain((d,)),
    }


def reference(x, c, ctx, c_ctx, w_mod, b_mod, norm1_g, norm2_g, w_in, mla_q_norm_g, mla_kv_norm_g,
              mla_w_uq, mla_w_ukv, ssm_conv_w, ssm_conv_b, ssm_dt_bias, ssm_a_log, ssm_d, ssm_norm_g,
              swa_sink, w_out, ffn_w_up, ffn_conv_w, ffn_conv_b, ffn_w_down, final_norm_g):
    s = x.shape[1]
    rows = s // GRID_W
    pos = (jnp.repeat(jnp.arange(rows), GRID_W), jnp.tile(jnp.arange(GRID_W), rows))
    silu_c = jax.nn.silu(c)
    silu_cc = jax.nn.silu(c_ctx)
    xc = ctx
    for l in range(DEPTH):
        mod = silu_c @ w_mod[l] + b_mod[l]
        modc = silu_cc @ w_mod[l] + b_mod[l]
        p = dict(norm1_g=norm1_g[l], norm2_g=norm2_g[l], w_in=w_in[l],
                 mla_q_norm_g=mla_q_norm_g[l], mla_kv_norm_g=mla_kv_norm_g[l],
                 mla_w_uq=mla_w_uq[l], mla_w_ukv=mla_w_ukv[l],
                 ssm_conv_w=ssm_conv_w[l], ssm_conv_b=ssm_conv_b[l], ssm_dt_bias=ssm_dt_bias[l],
                 ssm_a_log=ssm_a_log[l], ssm_d=ssm_d[l], ssm_norm_g=ssm_norm_g[l],
                 swa_sink=swa_sink[l], w_out=w_out[l], ffn_w_up=ffn_w_up[l],
                 ffn_conv_w=ffn_conv_w[l], ffn_conv_b=ffn_conv_b[l], ffn_w_down=ffn_w_down[l])
        x, xc = hybrid_layer(x, xc, mod, modc, p, pos, update_ctx=(l < DEPTH - 1))
    return rms_norm(x, final_norm_g)
```

```python
import functools

import jax
import jax.numpy as jnp
from jax import lax
from jax.experimental import pallas as pl
from jax.experimental.pallas import tpu as pltpu

F32 = jnp.float32
BF16 = jnp.bfloat16

NORM_EPS = 1e-6
GRID_W = 64
ROPE_BASE = 10000.0
NEG_INF = -1e30
LANE = 128
SUBLANE = 8
VMEM_LIMIT = 56 * 1024 * 1024

MLA_HEADS = 6
MLA_Q_RANK = 256
MLA_KV_RANK = 128
MLA_NOPE = 64
MLA_ROPE = 32
MLA_V = 64
SSM_HEADS = 6
SSM_HEAD_DIM = 64
SSM_D_INNER = SSM_HEADS * SSM_HEAD_DIM
SSM_GROUPS = 2
SSM_STATE = 128
SSM_CHUNK = 128
SSM_CONV_CH = SSM_D_INNER + 2 * SSM_GROUPS * SSM_STATE
SWA_HEADS = 4
SWA_KV_HEADS = 2
SWA_HEAD_DIM = 64
WINDOW = 128
BLOCK = 128
FFN_HIDDEN = 2816

IN_SPLITS = (MLA_Q_RANK, MLA_KV_RANK, MLA_ROPE, SSM_D_INNER, SSM_CONV_CH, 2 * SSM_HEADS,
             SWA_HEADS * SWA_HEAD_DIM, SWA_KV_HEADS * SWA_HEAD_DIM, SWA_KV_HEADS * SWA_HEAD_DIM)

O_QA = 0
O_KVA = O_QA + MLA_Q_RANK
O_KR = O_KVA + MLA_KV_RANK
O_Z = O_KR + LANE
O_XBC = O_Z + SSM_D_INNER
O_DT = O_XBC + SSM_CONV_CH
O_SWQ = O_DT + LANE
O_SWK = O_SWQ + SWA_HEADS * LANE
O_SWV = O_SWK + LANE
O_SWVR = O_SWV + LANE
IN_WIDTH_P = O_SWVR + LANE


def _cparams(sem):
    return pltpu.CompilerParams(dimension_semantics=sem, vmem_limit_bytes=VMEM_LIMIT)


def _rms(x, g):
    return x * lax.rsqrt(jnp.mean(x * x, axis=-1, keepdims=True) + NORM_EPS) * g


def _silu(x):
    return x * (1.0 / (1.0 + jnp.exp(-x)))


def _softplus(x):
    return jnp.maximum(x, 0.0) + jnp.log1p(jnp.exp(-jnp.abs(x)))


def _dot(a, b):
    return jnp.dot(a, b, preferred_element_type=F32)


def _dot_nt(a, b):
    return lax.dot_general(a, b, (((1,), (1,)), ((), ())), preferred_element_type=F32)


def _const_spec(shape):
    nd = len(shape)
    return pl.BlockSpec(shape, lambda *_: (0,) * nd)


def _mod_kernel(c_ref, w_ref, b_ref, o_ref):
    s = _silu(c_ref[...]).astype(BF16)
    o_ref[0] = _dot(s, w_ref[0].astype(BF16)) + b_ref[0]


def _modulation(cc, w_mod, b_mod):
    nl, d, n = w_mod.shape
    r = cc.shape[0]
    tn = 1536
    return pl.pallas_call(
        _mod_kernel,
        out_shape=jax.ShapeDtypeStruct((nl, r, n), F32),
        grid=(nl, n // tn),
        in_specs=[pl.BlockSpec((r, d), lambda l, j: (0, 0)),
                  pl.BlockSpec((1, d, tn), lambda l, j: (l, 0, j)),
                  pl.BlockSpec((1, 1, tn), lambda l, j: (l, 0, j))],
        out_specs=pl.BlockSpec((1, r, tn), lambda l, j: (l, 0, j)),
        compiler_params=_cparams(("arbitrary", "arbitrary")),
        name="modulation",
    )(cc, w_mod, b_mod.reshape(nl, 1, n))


def _rope(x, c, s, n, lane):
    first = (lane & (2 * n - 1)) < n
    p = jnp.where(first, pltpu.roll(x, LANE - n, 1), pltpu.roll(x, n, 1))
    return x * c + p * s


def _in_proj_kernel(x_ref, mod_ref, g1_ref, win_ref, gq_ref, gkv_ref, wuq_ref, wukv_ref,
                    cm_ref, sm_ref, cs_ref, ss_ref,
                    q_ref, k_ref, v_ref, z_ref, xbc_ref, dt_ref, swq_ref, swk_ref, swv_ref, swvr_ref):
    x = x_ref[0]
    tm = x.shape[0]
    mod = mod_ref[0]
    h = _rms(x, g1_ref[...]) * (1.0 + mod[1:2]) + mod[0:1]
    hb = h.astype(BF16)

    def seg(a, b):
        return _dot(hb, win_ref[:, a:b])

    lane = lax.broadcasted_iota(jnp.int32, (tm, LANE), 1)
    cm, sm, cs, ss = cm_ref[...], sm_ref[...], cs_ref[...], ss_ref[...]

    qn = _rms(seg(O_QA, O_KVA), gq_ref[...]).astype(BF16)
    qf = _dot(qn, wuq_ref[...])
    q_scale = (MLA_NOPE + MLA_ROPE) ** -0.5
    for hh in range(MLA_HEADS):
        qh = _rope(qf[:, hh * LANE:(hh + 1) * LANE], cm, sm, MLA_ROPE // 4, lane)
        q_ref[0, hh] = (qh * q_scale).astype(BF16)
    kn = _rms(seg(O_KVA, O_KR), gkv_ref[...]).astype(BF16)
    kvf = _dot(kn, wukv_ref[...])
    kr = _rope(seg(O_KR, O_Z), cm, sm, MLA_ROPE // 4, lane)
    for hh in range(MLA_HEADS):
        k_ref[0, hh] = (kvf[:, hh * LANE:(hh + 1) * LANE] + kr).astype(BF16)
        v_ref[0, hh] = kvf[:, (MLA_HEADS + hh) * LANE:(MLA_HEADS + hh + 1) * LANE].astype(BF16)

    z_ref[0] = seg(O_Z, O_XBC)
    xbc_ref[0] = seg(O_XBC, O_DT)
    dt_ref[0] = seg(O_DT, O_SWQ)

    sw_scale = SWA_HEAD_DIM ** -0.5
    for hh in range(SWA_HEADS):
        qh = _rope(seg(O_SWQ + hh * LANE, O_SWQ + (hh + 1) * LANE), cs, ss, SWA_HEAD_DIM // 4, lane)
        swq_ref[0, hh] = (qh * sw_scale).astype(BF16)
    swk_ref[0] = _rope(seg(O_SWK, O_SWV), cs, ss, SWA_HEAD_DIM // 4, lane).astype(BF16)
    swv_ref[0] = seg(O_SWV, O_SWVR).astype(BF16)
    swvr_ref[0] = seg(O_SWVR, IN_WIDTH_P).astype(BF16)


def _in_proj(x, mod, mod_batched, p, tables, tm):
    b, s, d = x.shape
    nt = s // tm
    mod_map = (lambda bi, i: (bi, 0, 0)) if mod_batched else (lambda bi, i: (0, 0, 0))
    tok = lambda w: pl.BlockSpec((1, tm, w), lambda bi, i: (bi, i, 0))
    heads = lambda nh: pl.BlockSpec((1, nh, tm, LANE), lambda bi, i: (bi, 0, i, 0))
    tab = pl.BlockSpec((tm, LANE), lambda bi, i: (i, 0))
    out_shape = (
        jax.ShapeDtypeStruct((b, MLA_HEADS, s, LANE), BF16),
        jax.ShapeDtypeStruct((b, MLA_HEADS, s, LANE), BF16),
        jax.ShapeDtypeStruct((b, MLA_HEADS, s, LANE), BF16),
        jax.ShapeDtypeStruct((b, s, SSM_D_INNER), F32),
        jax.ShapeDtypeStruct((b, s, SSM_CONV_CH), F32),
        jax.ShapeDtypeStruct((b, s, LANE), F32),
        jax.ShapeDtypeStruct((b, SWA_HEADS, s, LANE), BF16),
        jax.ShapeDtypeStruct((b, s, LANE), BF16),
        jax.ShapeDtypeStruct((b, s, LANE), BF16),
        jax.ShapeDtypeStruct((b, s, LANE), BF16),
    )
    out_specs = (heads(MLA_HEADS), heads(MLA_HEADS), heads(MLA_HEADS), tok(SSM_D_INNER), tok(SSM_CONV_CH),
                 tok(LANE), heads(SWA_HEADS), tok(LANE), tok(LANE), tok(LANE))
    return pl.pallas_call(
        _in_proj_kernel,
        out_shape=out_shape,
        grid=(b, nt),
        in_specs=[tok(d),
                  pl.BlockSpec((1, 6, d), mod_map),
                  _const_spec((1, d)),
                  _const_spec(p['w_in'].shape),
                  _const_spec((1, MLA_Q_RANK)),
                  _const_spec((1, MLA_KV_RANK)),
                  _const_spec(p['w_uq'].shape),
                  _const_spec(p['w_ukv'].shape),
                  tab, tab, tab, tab],
        out_specs=out_specs,
        compiler_params=_cparams(("parallel", "parallel")),
        name="in_proj",
    )(x, mod, p['norm1_g'], p['w_in'], p['gq'], p['gkv'], p['w_uq'], p['w_ukv'], *tables)


def _mla_kernel(*refs, has_latent, tk):
    if has_latent:
        q_ref, kc_ref, vc_ref, k_ref, v_ref, o_ref = refs
    else:
        q_ref, kc_ref, vc_ref, o_ref = refs
    out = None
    for hh in range(2):
        q = q_ref[0, hh]
        s = _dot_nt(q, kc_ref[0, hh])
        m = jnp.max(s, axis=-1, keepdims=True)
        p = jnp.exp(s - m)
        l = jnp.sum(p, axis=-1, keepdims=True)
        acc = _dot(p.astype(BF16), vc_ref[0, hh])
        if has_latent:
            nk = k_ref.shape[2] // tk

            def body(j, carry, q=q, hh=hh):
                m, l, acc = carry
                off = pl.multiple_of(j * tk, tk)
                kj = k_ref[0, hh, pl.ds(off, tk), :]
                vj = v_ref[0, hh, pl.ds(off, tk), :]
                s = _dot_nt(q, kj)
                m_new = jnp.maximum(m, jnp.max(s, axis=-1, keepdims=True))
                alpha = jnp.exp(m - m_new)
                p = jnp.exp(s - m_new)
                l = alpha * l + jnp.sum(p, axis=-1, keepdims=True)
                acc = alpha * acc + _dot(p.astype(BF16), vj)
                return m_new, l, acc

            m, l, acc = lax.fori_loop(0, nk, body, (m, l, acc))
        oh = acc / l
        out = oh if out is None else out + oh
    o_ref[0] = out


def _mla(q, kc, vc, k, v, tq, tk):
    b, nh, s, _ = q.shape
    c = kc.shape[2]
    has_latent = k is not None
    pair = lambda n: pl.BlockSpec((1, 2, n, LANE), lambda bi, hp, i: (bi, hp, 0, 0))
    in_specs = [pl.BlockSpec((1, 2, tq, LANE), lambda bi, hp, i: (bi, hp, i, 0)), pair(c), pair(c)]
    args = [q, kc, vc]
    if has_latent:
        in_specs += [pair(s), pair(s)]
        args += [k, v]
    return pl.pallas_call(
        functools.partial(_mla_kernel, has_latent=has_latent, tk=tk),
        out_shape=jax.ShapeDtypeStruct((b, s, nh * MLA_V), F32),
        grid=(b, nh // 2, s // tq),
        in_specs=in_specs,
        out_specs=pl.BlockSpec((1, tq, LANE), lambda bi, hp, i: (bi, i, hp)),
        compiler_params=_cparams(("parallel", "parallel", "arbitrary")),
        name="mla_latent" if has_latent else "mla_context",
    )(*args)


def _swa_kernel(*refs, has_band, seq):
    if has_band:
        (sink_ref, q_ref, kc_ref, vc_ref, vcr_ref, kp_ref, k0_ref, kn_ref,
         vp_ref, v0_ref, vn_ref, vrp_ref, vr0_ref, vrn_ref, o_ref) = refs
    else:
        sink_ref, q_ref, kc_ref, vc_ref, vcr_ref, o_ref = refs
    tq = q_ref.shape[2]
    i = pl.program_id(1)
    half = lax.broadcasted_iota(jnp.int32, (tq, LANE), 1) // SWA_HEAD_DIM
    kc = kc_ref[0]
    if has_band:
        kb = jnp.concatenate([kp_ref[0], k0_ref[0], kn_ref[0]], axis=0)
        vb = jnp.concatenate([vp_ref[0], v0_ref[0], vn_ref[0]], axis=0)
        vrb = jnp.concatenate([vrp_ref[0], vr0_ref[0], vrn_ref[0]], axis=0)
        qpos = i * tq + lax.broadcasted_iota(jnp.int32, (tq, 3 * tq), 0)
        kpos = (i - 1) * tq + lax.broadcasted_iota(jnp.int32, (tq, 3 * tq), 1)
        valid = (jnp.abs(kpos - qpos) <= WINDOW) & (kpos >= 0) & (kpos < seq)
    for g in range(SWA_KV_HEADS):
        og = None
        for r in range(SWA_HEADS // SWA_KV_HEADS):
            hh = g * (SWA_HEADS // SWA_KV_HEADS) + r
            same = (g == r)
            q = q_ref[0, hh]
            sink = sink_ref[hh]
            sc = _dot_nt(q, kc)
            m = jnp.maximum(jnp.max(sc, axis=-1, keepdims=True), sink)
            if has_band:
                sb = jnp.where(valid, _dot_nt(q, kb), NEG_INF)
                m = jnp.maximum(m, jnp.max(sb, axis=-1, keepdims=True))
            pc = jnp.exp(sc - m)
            l = jnp.sum(pc, axis=-1, keepdims=True) + jnp.exp(sink - m)
            o = _dot(pc.astype(BF16), (vc_ref if same else vcr_ref)[0])
            if has_band:
                pb = jnp.exp(sb - m)
                l = l + jnp.sum(pb, axis=-1, keepdims=True)
                o = o + _dot(pb.astype(BF16), vb if same else vrb)
            o = jnp.where(half == r, o / l, 0.0)
            og = o if og is None else og + o
        o_ref[0, :, g * LANE:(g + 1) * LANE] = og


def _swa(sink, q, kc, vc, vcr, k, v, vr):
    b, nh, s, _ = q.shape
    c = kc.shape[1]
    tq = BLOCK
    nb = s // tq
    has_band = k is not None
    ctx_spec = pl.BlockSpec((1, c, LANE), lambda bi, i: (bi, 0, 0))
    in_specs = [pl.BlockSpec(memory_space=pltpu.SMEM),
                pl.BlockSpec((1, nh, tq, LANE), lambda bi, i: (bi, 0, i, 0)),
                ctx_spec, ctx_spec, ctx_spec]
    args = [sink, q, kc, vc, vcr]
    if has_band:
        prev = pl.BlockSpec((1, tq, LANE), lambda bi, i: (bi, jnp.maximum(i - 1, 0), 0))
        cur = pl.BlockSpec((1, tq, LANE), lambda bi, i: (bi, i, 0))
        nxt = pl.BlockSpec((1, tq, LANE), lambda bi, i: (bi, jnp.minimum(i + 1, nb - 1), 0))
        in_specs += [prev, cur, nxt] * 3
        args += [k, k, k, v, v, v, vr, vr, vr]
    return pl.pallas_call(
        functools.partial(_swa_kernel, has_band=has_band, seq=s),
        out_shape=jax.ShapeDtypeStruct((b, s, nh * SWA_HEAD_DIM), F32),
        grid=(b, nb),
        in_specs=in_specs,
        out_specs=pl.BlockSpec((1, tq, nh * SWA_HEAD_DIM), lambda bi, i: (bi, i, 0)),
        compiler_params=_cparams(("parallel", "parallel")),
        name="swa_latent" if has_band else "swa_context",
    )(*args)


def _split_dot(t_bf16, v):
    hi = v.astype(BF16)
    r1 = v - hi.astype(F32)
    mid = r1.astype(BF16)
    lo = (r1 - mid.astype(F32)).astype(BF16)
    return _dot(t_bf16, hi) + _dot(t_bf16, mid) + _dot(t_bf16, lo)


def _ssd_kernel(xf_ref, xfp_ref, xfn_ref, xb_ref, xbp_ref, xbn_ref, dtf_ref, dtb_ref,
                cw_ref, cb_ref, dtbias_ref, alog_ref, dskip_ref, h0_ref,
                yf_ref, yb_ref, hfin_ref, hst_ref):
    c = pl.program_id(1)
    nc = pl.num_programs(1)
    q = SSM_CHUNK
    gn = SSM_GROUPS * SSM_STATE

    @pl.when(c == 0)
    def _():
        hst_ref[...] = h0_ref[0]

    row = lax.broadcasted_iota(jnp.int32, (q, q), 0)
    col = lax.broadcasted_iota(jnp.int32, (q, q), 1)
    rid = lax.broadcasted_iota(jnp.int32, (q, 1), 0)
    lane = lax.broadcasted_iota(jnp.int32, (1, LANE), 1)
    low_half = lane < SSM_HEAD_DIM
    srow = lax.broadcasted_iota(jnp.int32, (gn, SSM_D_INNER), 0) // SSM_STATE
    scol = lax.broadcasted_iota(jnp.int32, (gn, SSM_D_INNER), 1) // (SSM_D_INNER // SSM_GROUPS)
    gmask = srow == scol
    cw = cw_ref[...]
    a_neg = -jnp.exp(alog_ref[...])

    for d, (x_ref, xp_ref, xn_ref, dt_ref, y_ref) in enumerate(
            ((xf_ref, xfp_ref, xfn_ref, dtf_ref, yf_ref), (xb_ref, xbp_ref, xbn_ref, dtb_ref, yb_ref))):
        cd = c if d == 0 else nc - 1 - c
        keep = (col <= row) if d == 0 else (col >= row)
        x = x_ref[0]
        prev_row = jnp.where(cd > 0, xp_ref[0][SUBLANE - 1:SUBLANE], 0.0)
        next_row = jnp.where(cd < nc - 1, xn_ref[0][0:1], 0.0)
        xm1 = jnp.where(rid == 0, prev_row, pltpu.roll(x, 1, 0))
        xp1 = jnp.where(rid == q - 1, next_row, pltpu.roll(x, q - 1, 0))
        xc = _silu(cw[0:1] * xm1 + cw[1:2] * x + cw[2:3] * xp1 + cb_ref[...])
        xs = xc[:, :SSM_D_INNER]
        bm = xc[:, SSM_D_INNER:SSM_D_INNER + gn]
        cm = xc[:, SSM_D_INNER + gn:]

        dt = _softplus(dt_ref[0] + dtbias_ref[...])
        acs = _split_dot(keep.astype(BF16), dt * a_neg)
        tot = acs[q - 1:q] if d == 0 else acs[0:1]
        acs_t = jnp.transpose(acs)
        e_in = jnp.exp(acs)
        e_out = jnp.exp(tot - acs)
        e_tot = jnp.exp(tot)

        def expand(v, d=d):
            blocks = []
            for bb in range(SSM_HEADS // 2):
                j = d * SSM_HEADS + 2 * bb
                blocks.append(jnp.where(low_half, v[:, j:j + 1], v[:, j + 1:j + 2]))
            return jnp.concatenate(blocks, axis=1)

        xdt = xs * expand(dt)
        cb16 = cm.astype(BF16)
        bb16 = bm.astype(BF16)
        hs = hst_ref[d]
        y_off = _dot(cb16, hs.astype(BF16)) * expand(e_in)
        bt16 = jnp.transpose(bm).astype(BF16)
        states = _dot(bt16, (xdt * expand(e_out)).astype(BF16))
        hst_ref[d] = hs * expand(e_tot) + jnp.where(gmask, states, 0.0)

        cbg = [_dot_nt(cb16[:, g * SSM_STATE:(g + 1) * SSM_STATE], bb16[:, g * SSM_STATE:(g + 1) * SSM_STATE])
               for g in range(SSM_GROUPS)]
        y_blocks = []
        for bb in range(SSM_HEADS // 2):
            xblk = xdt[:, bb * LANE:(bb + 1) * LANE]
            yb = None
            for r in range(2):
                hh = 2 * bb + r
                g = hh // (SSM_HEADS // SSM_GROUPS)
                j = d * SSM_HEADS + hh
                diff = acs[:, j:j + 1] - acs_t[j:j + 1, :]
                decay = jnp.exp(jnp.where(keep, diff, NEG_INF))
                mh = (cbg[g] * decay).astype(BF16)
                xh = jnp.where(low_half if r == 0 else jnp.logical_not(low_half), xblk, 0.0).astype(BF16)
                t = _dot(mh, xh)
                yb = t if yb is None else yb + t
            y_blocks.append(yb)
        y = jnp.concatenate(y_blocks, axis=1) + y_off
        if d == 0:
            y = y + dskip_ref[...] * xs
        y_ref[0] = y

    @pl.when(c == nc - 1)
    def _():
        hfin_ref[0] = hst_ref[...]


def _ssd(xbc, dt, h0, p):
    b, s, _ = xbc.shape
    q = SSM_CHUNK
    nc = s // q
    r8 = q // SUBLANE
    n8 = s // SUBLANE
    gn = SSM_GROUPS * SSM_STATE
    fw = lambda bi, c: (bi, c, 0)
    bw = lambda bi, c: (bi, nc - 1 - c, 0)
    fw_p = lambda bi, c: (bi, jnp.maximum(c * r8 - 1, 0), 0)
    fw_n = lambda bi, c: (bi, jnp.minimum((c + 1) * r8, n8 - 1), 0)
    bw_p = lambda bi, c: (bi, jnp.maximum((nc - 1 - c) * r8 - 1, 0), 0)
    bw_n = lambda bi, c: (bi, jnp.minimum((nc - c) * r8, n8 - 1), 0)
    ch = SSM_CONV_CH
    state_spec = pl.BlockSpec((1, 2, gn, SSM_D_INNER), lambda bi, c: (bi, 0, 0, 0))
    return pl.pallas_call(
        _ssd_kernel,
        out_shape=(jax.ShapeDtypeStruct((b, s, SSM_D_INNER), F32),
                   jax.ShapeDtypeStruct((b, s, SSM_D_INNER), F32),
                   jax.ShapeDtypeStruct((b, 2, gn, SSM_D_INNER), F32)),
        grid=(b, nc),
        in_specs=[pl.BlockSpec((1, q, ch), fw), pl.BlockSpec((1, SUBLANE, ch), fw_p), pl.BlockSpec((1, SUBLANE, ch), fw_n),
                  pl.BlockSpec((1, q, ch), bw), pl.BlockSpec((1, SUBLANE, ch), bw_p), pl.BlockSpec((1, SUBLANE, ch), bw_n),
                  pl.BlockSpec((1, q, LANE), fw), pl.BlockSpec((1, q, LANE), bw),
                  _const_spec((3, ch)), _const_spec((1, ch)), _const_spec((1, LANE)), _const_spec((1, LANE)),
                  _const_spec((1, SSM_D_INNER)), state_spec],
        out_specs=(pl.BlockSpec((1, q, SSM_D_INNER), fw), pl.BlockSpec((1, q, SSM_D_INNER), bw), state_spec),
        scratch_shapes=[pltpu.VMEM((2, gn, SSM_D_INNER), F32)],
        compiler_params=_cparams(("parallel", "arbitrary")),
        name="ssd_scan",
    )(xbc, xbc, xbc, xbc, xbc, xbc, dt, dt, p['conv_w'], p['conv_b'], p['dt_bias'], p['a_log'], p['d_skip'], h0)


def _out_proj_kernel(ya_ref, yf_ref, yb_ref, z_ref, yc_ref, x_ref, mod_ref, gn_ref, w_ref, o_ref):
    y = (yf_ref[0] + yb_ref[0]) * _silu(z_ref[0])
    ybn = _rms(y, gn_ref[...])
    na = ya_ref.shape[2]
    nb = na + ybn.shape[1]
    o = (_dot(ya_ref[0].astype(BF16), w_ref[0:na, :])
         + _dot(ybn.astype(BF16), w_ref[na:nb, :])
         + _dot(yc_ref[0].astype(BF16), w_ref[nb:, :]))
    o_ref[0] = x_ref[0] + mod_ref[0][2:3] * o


def _out_proj(ya, yf, yb, z, yc, x, mod, mod_batched, p, tm):
    b, s, d = x.shape
    mod_map = (lambda bi, i: (bi, 0, 0)) if mod_batched else (lambda bi, i: (0, 0, 0))
    tok = lambda w: pl.BlockSpec((1, tm, w), lambda bi, i: (bi, i, 0))
    return pl.pallas_call(
        _out_proj_kernel,
        out_shape=jax.ShapeDtypeStruct((b, s, d), F32),
        grid=(b, s // tm),
        in_specs=[tok(ya.shape[2]), tok(yf.shape[2]), tok(yb.shape[2]), tok(z.shape[2]), tok(yc.shape[2]), tok(d),
                  pl.BlockSpec((1, 6, d), mod_map), _const_spec((1, SSM_D_INNER)), _const_spec(p['w_out'].shape)],
        out_specs=tok(d),
        compiler_params=_cparams(("parallel", "parallel")),
        name="out_proj",
    )(ya, yf, yb, z, yc, x, mod, p['ssm_norm_g'], p['w_out'])


def _ffn_kernel(*refs, final_norm, tf):
    if final_norm:
        x_ref, xp_ref, xn_ref, mod_ref, g2_ref, wup_ref, cw_ref, cb_ref, wdn_ref, gfin_ref, o_ref, act_ref = refs
    else:
        x_ref, xp_ref, xn_ref, mod_ref, g2_ref, wup_ref, cw_ref, cb_ref, wdn_ref, o_ref, act_ref = refs
    i = pl.program_id(1)
    nt = pl.num_programs(1)
    x = x_ref[0]
    tm = x.shape[0]
    te = tm + 2 * SUBLANE
    f = wdn_ref.shape[0]
    mod = mod_ref[0]
    xe = jnp.concatenate([xp_ref[0], x, xn_ref[0]], axis=0)
    he = (_rms(xe, g2_ref[...]) * (1.0 + mod[4:5]) + mod[3:4]).astype(BF16)
    hc = he[SUBLANE:SUBLANE + tm]
    rid = lax.broadcasted_iota(jnp.int32, (te, 1), 0)
    inside = ((rid >= SUBLANE) | (i > 0)) & ((rid < tm + SUBLANE) | (i < nt - 1))
    for j in range(f // tf):
        val = _dot(hc, wup_ref[:, j * tf:(j + 1) * tf])
        gate = jnp.where(inside, _dot(he, wup_ref[:, f + j * tf:f + (j + 1) * tf]), 0.0)
        cw = cw_ref[:, j * tf:(j + 1) * tf]
        gm1 = pltpu.roll(gate, 1, 0)[SUBLANE:SUBLANE + tm]
        gp1 = pltpu.roll(gate, te - 1, 0)[SUBLANE:SUBLANE + tm]
        gc = cw[0:1] * gm1 + cw[1:2] * gate[SUBLANE:SUBLANE + tm] + cw[2:3] * gp1 + cb_ref[:, j * tf:(j + 1) * tf]
        act_ref[:, j * tf:(j + 1) * tf] = (_silu(gc) * val).astype(BF16)
    out = x + mod[5:6] * _dot(act_ref[...], wdn_ref[...])
    if final_norm:
        out = _rms(out, gfin_ref[...])
    o_ref[0] = out


def _ffn(x, mod, mod_batched, p, gfin, tm):
    b, s, d = x.shape
    f = p['w_down'].shape[0]
    nt = s // tm
    r8 = tm // SUBLANE
    n8 = s // SUBLANE
    mod_map = (lambda bi, i: (bi, 0, 0)) if mod_batched else (lambda bi, i: (0, 0, 0))
    final_norm = gfin is not None
    in_specs = [pl.BlockSpec((1, tm, d), lambda bi, i: (bi, i, 0)),
                pl.BlockSpec((1, SUBLANE, d), lambda bi, i: (bi, jnp.maximum(i * r8 - 1, 0), 0)),
                pl.BlockSpec((1, SUBLANE, d), lambda bi, i: (bi, jnp.minimum((i + 1) * r8, n8 - 1), 0)),
                pl.BlockSpec((1, 6, d), mod_map), _const_spec((1, d)),
                _const_spec(p['w_up'].shape), _const_spec((3, f)), _const_spec((1, f)),
                _const_spec(p['w_down'].shape)]
    args = [x, x, x, mod, p['norm2_g'], p['w_up'], p['ffn_conv_w'], p['ffn_conv_b'], p['w_down']]
    if final_norm:
        in_specs.append(_const_spec((1, d)))
        args.append(gfin)
    return pl.pallas_call(
        functools.partial(_ffn_kernel, final_norm=final_norm, tf=256),
        out_shape=jax.ShapeDtypeStruct((b, s, d), F32),
        grid=(b, nt),
        in_specs=in_specs,
        out_specs=pl.BlockSpec((1, tm, d), lambda bi, i: (bi, i, 0)),
        scratch_shapes=[pltpu.VMEM((tm, f), BF16)],
        compiler_params=_cparams(("parallel", "parallel")),
        name="ffn_final" if final_norm else "ffn",
    )(*args)


def _rope_tables(s):
    t = jnp.arange(s)
    row = (t // GRID_W).astype(F32)[:, None]
    col = (t % GRID_W).astype(F32)[:, None]

    def cs(n):
        inv = jnp.power(ROPE_BASE, -jnp.arange(n, dtype=F32) / n)
        ar, ac = row * inv, col * inv
        cos = jnp.concatenate([jnp.cos(ar), jnp.cos(ar), jnp.cos(ac), jnp.cos(ac)], axis=1)
        sin = jnp.concatenate([-jnp.sin(ar), jnp.sin(ar), -jnp.sin(ac), jnp.sin(ac)], axis=1)
        return cos, sin

    c32, s32 = cs(MLA_ROPE // 4)
    c64, s64 = cs(SWA_HEAD_DIM // 4)
    ones = jnp.ones((s, LANE), F32)
    zeros = jnp.zeros((s, LANE), F32)
    cm = ones.at[:, MLA_NOPE:MLA_NOPE + MLA_ROPE].set(c32)
    sm = zeros.at[:, MLA_NOPE:MLA_NOPE + MLA_ROPE].set(s32)
    return cm, sm, jnp.concatenate([c64, c64], axis=1), jnp.concatenate([s64, s64], axis=1)


def _identity_tables(s):
    ones = jnp.ones((s, LANE), F32)
    zeros = jnp.zeros((s, LANE), F32)
    return ones, zeros, ones, zeros


def _layer_params(l, w):
    d = w['w_in'].shape[1]
    w_in = w['w_in'][l]
    idx = [0]
    for n in IN_SPLITS:
        idx.append(idx[-1] + n)
    qa, kva, kr, z, xbc, dtr, swq, swk, swv = [w_in[:, idx[i]:idx[i + 1]] for i in range(len(IN_SPLITS))]
    zc = lambda n: jnp.zeros((d, n), F32)
    hd = SWA_HEAD_DIM
    swq_blocks = []
    for hh in range(SWA_HEADS):
        qh = swq[:, hh * hd:(hh + 1) * hd]
        swq_blocks += [qh, zc(hd)] if hh < SWA_HEADS // SWA_KV_HEADS else [zc(hd), qh]
    w_in_p = jnp.concatenate(
        [qa, kva, zc(MLA_NOPE), kr, zc(LANE - MLA_NOPE - MLA_ROPE), z, xbc, dtr, zc(LANE - 2 * SSM_HEADS)]
        + swq_blocks + [swk, swv, swv[:, hd:], swv[:, :hd]], axis=1).astype(BF16)

    w_uq = w['mla_w_uq'][l].reshape(MLA_Q_RANK, MLA_HEADS, MLA_NOPE + MLA_ROPE)
    w_uq_p = jnp.pad(w_uq, ((0, 0), (0, 0), (0, LANE - MLA_NOPE - MLA_ROPE))).reshape(MLA_Q_RANK, MLA_HEADS * LANE)
    w_ukv = w['mla_w_ukv'][l].reshape(MLA_KV_RANK, MLA_HEADS, MLA_NOPE + MLA_V)
    wk = jnp.pad(w_ukv[:, :, :MLA_NOPE], ((0, 0), (0, 0), (0, LANE - MLA_NOPE)))
    wv = w_ukv[:, :, MLA_NOPE:]
    zv = jnp.zeros_like(wv)
    even = (jnp.arange(MLA_HEADS) % 2 == 0)[None, :, None]
    wv_p = jnp.concatenate([jnp.where(even, wv, zv), jnp.where(even, zv, wv)], axis=2)
    w_ukv_p = jnp.concatenate([wk.reshape(MLA_KV_RANK, -1), wv_p.reshape(MLA_KV_RANK, -1)], axis=1)

    pad_row = lambda v: jnp.pad(v.reshape(1, -1), ((0, 0), (0, LANE - v.size)))
    return dict(
        norm1_g=w['norm1_g'][l][None], norm2_g=w['norm2_g'][l][None],
        w_in=w_in_p, gq=w['mla_q_norm_g'][l][None], gkv=w['mla_kv_norm_g'][l][None],
        w_uq=w_uq_p.astype(BF16), w_ukv=w_ukv_p.astype(BF16),
        conv_w=w['ssm_conv_w'][l], conv_b=w['ssm_conv_b'][l][None],
        dt_bias=pad_row(w['ssm_dt_bias'][l]), a_log=pad_row(w['ssm_a_log'][l]),
        d_skip=jnp.repeat(w['ssm_d'][l], SSM_HEAD_DIM)[None], ssm_norm_g=w['ssm_norm_g'][l][None],
        sink=w['swa_sink'][l], w_out=w['w_out'][l].astype(BF16),
        w_up=w['ffn_w_up'][l].astype(BF16), ffn_conv_w=w['ffn_conv_w'][l], ffn_conv_b=w['ffn_conv_b'][l][None],
        w_down=w['ffn_w_down'][l].astype(BF16),
    )


def kernel(x, c, ctx, c_ctx, w_mod, b_mod, norm1_g, norm2_g, w_in, mla_q_norm_g, mla_kv_norm_g, mla_w_uq,
           mla_w_ukv, ssm_conv_w, ssm_conv_b, ssm_dt_bias, ssm_a_log, ssm_d, ssm_norm_g, swa_sink, w_out,
           ffn_w_up, ffn_conv_w, ffn_conv_b, ffn_w_down, final_norm_g):
    w = dict(w_in=w_in, norm1_g=norm1_g, norm2_g=norm2_g, mla_q_norm_g=mla_q_norm_g, mla_kv_norm_g=mla_kv_norm_g,
             mla_w_uq=mla_w_uq, mla_w_ukv=mla_w_ukv, ssm_conv_w=ssm_conv_w, ssm_conv_b=ssm_conv_b,
             ssm_dt_bias=ssm_dt_bias, ssm_a_log=ssm_a_log, ssm_d=ssm_d, ssm_norm_g=ssm_norm_g, swa_sink=swa_sink,
             w_out=w_out, ffn_w_up=ffn_w_up, ffn_conv_w=ffn_conv_w, ffn_conv_b=ffn_conv_b, ffn_w_down=ffn_w_down)
    b, s, d = x.shape
    n_ctx = ctx.shape[1]
    depth = w_mod.shape[0]
    tm = min(512, s)
    tmc = min(512, n_ctx)

    rows = 2 * SUBLANE
    cc = jnp.concatenate([c, c_ctx[None], jnp.zeros((rows - b - 1, d), F32)], axis=0)
    mods = _modulation(cc, w_mod, b_mod).reshape(depth, rows, 6, d)

    lat_tables = _rope_tables(s)
    ctx_tables = _identity_tables(n_ctx)
    h_zero = jnp.zeros((b, 2, SSM_GROUPS * SSM_STATE, SSM_D_INNER), F32)

    xc = ctx
    for l in range(depth):
        p = _layer_params(l, w)
        mod = mods[l, :b]
        modc = mods[l, b:b + 1]
        last = l == depth - 1

        qc, kc, vc, zc, xbcc, dtc, swqc, swkc, swvc, swvrc = _in_proj(xc, modc, False, p, ctx_tables, tmc)
        q, k, v, z, xbc, dt, swq, swk, swv, swvr = _in_proj(x, mod, True, p, lat_tables, tm)

        ya = _mla(q, kc, vc, k, v, tq=min(256, s), tk=min(512, s))
        yfc, ybc, h_ctx = _ssd(xbcc, dtc, h_zero, p)
        yf, yb, _ = _ssd(xbc, dt, h_ctx, p)
        yc = _swa(p['sink'], swq, swkc, swvc, swvrc, swk, swv, swvr)

        x = _out_proj(ya, yf, yb, z, yc, x, mod, True, p, tm)
        x = _ffn(x, mod, True, p, final_norm_g[None] if last else None, tm)

        if not last:
            yac = _mla(qc, kc, vc, None, None, tq=min(256, n_ctx), tk=None)
            ycc = _swa(p['sink'], swqc, swkc, swvc, swvrc, None, None, None)
            xc = _out_proj(yac, yfc, ybc, zc, ycc, xc, modc, False, p, tmc)
            xc = _ffn(xc, modc, False, p, None, tmc)
    return x
```

```python
import functools

import jax
import jax.numpy as jnp
from jax import lax
from jax.experimental import pallas as pl
from jax.experimental.pallas import tpu as pltpu

F32 = jnp.float32
BF16 = jnp.bfloat16

NORM_EPS = 1e-6
GRID_W = 64
ROPE_BASE = 10000.0
NEG_INF = -1e30
LANE = 128
SUBLANE = 8
VMEM_LIMIT = 56 * 1024 * 1024

MLA_HEADS = 6
MLA_Q_RANK = 256
MLA_KV_RANK = 128
MLA_NOPE = 64
MLA_ROPE = 32
MLA_V = 64
SSM_HEADS = 6
SSM_HEAD_DIM = 64
SSM_D_INNER = SSM_HEADS * SSM_HEAD_DIM
SSM_GROUPS = 2
SSM_STATE = 128
SSM_CHUNK = 128
SSM_CONV_CH = SSM_D_INNER + 2 * SSM_GROUPS * SSM_STATE
SWA_HEADS = 4
SWA_KV_HEADS = 2
SWA_HEAD_DIM = 64
WINDOW = 128
BLOCK = 128
FFN_HIDDEN = 2816

IN_SPLITS = (MLA_Q_RANK, MLA_KV_RANK, MLA_ROPE, SSM_D_INNER, SSM_CONV_CH, 2 * SSM_HEADS,
             SWA_HEADS * SWA_HEAD_DIM, SWA_KV_HEADS * SWA_HEAD_DIM, SWA_KV_HEADS * SWA_HEAD_DIM)

O_QA = 0
O_KVA = O_QA + MLA_Q_RANK
O_KR = O_KVA + MLA_KV_RANK
O_Z = O_KR + LANE
O_XBC = O_Z + SSM_D_INNER
O_DT = O_XBC + SSM_CONV_CH
O_SWQ = O_DT + LANE
O_SWK = O_SWQ + SWA_HEADS * LANE
O_SWV = O_SWK + LANE
O_SWVR = O_SWV + LANE
IN_WIDTH_P = O_SWVR + LANE


def _cparams(sem):
    return pltpu.CompilerParams(dimension_semantics=sem, vmem_limit_bytes=VMEM_LIMIT)


def _rms(x, g):
    return x * lax.rsqrt(jnp.mean(x * x, axis=-1, keepdims=True) + NORM_EPS) * g


def _silu(x):
    return x * (1.0 / (1.0 + jnp.exp(-x)))


def _softplus(x):
    return jnp.maximum(x, 0.0) + jnp.log1p(jnp.exp(-jnp.abs(x)))


def _dot(a, b):
    return jnp.dot(a, b, preferred_element_type=F32)


def _dot_nt(a, b):
    return lax.dot_general(a, b, (((1,), (1,)), ((), ())), preferred_element_type=F32)


def _const_spec(shape):
    nd = len(shape)
    return pl.BlockSpec(shape, lambda *_: (0,) * nd)


def _mod_kernel(c_ref, w_ref, b_ref, o_ref):
    s = _silu(c_ref[...]).astype(BF16)
    o_ref[0] = _dot(s, w_ref[0].astype(BF16)) + b_ref[0]


def _modulation(cc, w_mod, b_mod):
    nl, d, n = w_mod.shape
    r = cc.shape[0]
    tn = 1536
    return pl.pallas_call(
        _mod_kernel,
        out_shape=jax.ShapeDtypeStruct((nl, r, n), F32),
        grid=(nl, n // tn),
        in_specs=[pl.BlockSpec((r, d), lambda l, j: (0, 0)),
                  pl.BlockSpec((1, d, tn), lambda l, j: (l, 0, j)),
                  pl.BlockSpec((1, 1, tn), lambda l, j: (l, 0, j))],
        out_specs=pl.BlockSpec((1, r, tn), lambda l, j: (l, 0, j)),
        compiler_params=_cparams(("arbitrary", "arbitrary")),
        name="modulation",
    )(cc, w_mod, b_mod.reshape(nl, 1, n))


def _rope(x, c, s, n, lane):
    first = (lane & (2 * n - 1)) < n
    p = jnp.where(first, pltpu.roll(x, LANE - n, 1), pltpu.roll(x, n, 1))
    return x * c + p * s


def _in_proj_kernel(x_ref, mod_ref, g1_ref, win_ref, gq_ref, gkv_ref, wuq_ref, wukv_ref,
                    cm_ref, sm_ref, cs_ref, ss_ref,
                    q_ref, k_ref, v_ref, z_ref, xbc_ref, dt_ref, swq_ref, swk_ref, swv_ref, swvr_ref):
    x = x_ref[0]
    tm = x.shape[0]
    mod = mod_ref[0]
    h = _rms(x, g1_ref[...]) * (1.0 + mod[1:2]) + mod[0:1]
    hb = h.astype(BF16)

    def seg(a, b):
        return _dot(hb, win_ref[:, a:b])

    lane = lax.broadcasted_iota(jnp.int32, (tm, LANE), 1)
    cm, sm, cs, ss = cm_ref[...], sm_ref[...], cs_ref[...], ss_ref[...]

    qn = _rms(seg(O_QA, O_KVA), gq_ref[...]).astype(BF16)
    qf = _dot(qn, wuq_ref[...])
    q_scale = (MLA_NOPE + MLA_ROPE) ** -0.5
    for hh in range(MLA_HEADS):
        qh = _rope(qf[:, hh * LANE:(hh + 1) * LANE], cm, sm, MLA_ROPE // 4, lane)
        q_ref[0, hh] = (qh * q_scale).astype(BF16)
    kn = _rms(seg(O_KVA, O_KR), gkv_ref[...]).astype(BF16)
    kvf = _dot(kn, wukv_ref[...])
    kr = _rope(seg(O_KR, O_Z), cm, sm, MLA_ROPE // 4, lane)
    for hh in range(MLA_HEADS):
        k_ref[0, hh] = (kvf[:, hh * LANE:(hh + 1) * LANE] + kr).astype(BF16)
        v_ref[0, hh] = kvf[:, (MLA_HEADS + hh) * LANE:(MLA_HEADS + hh + 1) * LANE].astype(BF16)

    z_ref[0] = seg(O_Z, O_XBC)
    xbc_ref[0] = seg(O_XBC, O_DT)
    dt_ref[0] = seg(O_DT, O_SWQ)

    sw_scale = SWA_HEAD_DIM ** -0.5
    for hh in range(SWA_HEADS):
        qh = _rope(seg(O_SWQ + hh * LANE, O_SWQ + (hh + 1) * LANE), cs, ss, SWA_HEAD_DIM // 4, lane)
        swq_ref[0, hh] = (qh * sw_scale).astype(BF16)
    swk_ref[0] = _rope(seg(O_SWK, O_SWV), cs, ss, SWA_HEAD_DIM // 4, lane).astype(BF16)
    swv_ref[0] = seg(O_SWV, O_SWVR).astype(BF16)
    swvr_ref[0] = seg(O_SWVR, IN_WIDTH_P).astype(BF16)


def _in_proj(x, mod, mod_batched, p, tables, tm):
    b, s, d = x.shape
    nt = s // tm
    mod_map = (lambda bi, i: (bi, 0, 0)) if mod_batched else (lambda bi, i: (0, 0, 0))
    tok = lambda w: pl.BlockSpec((1, tm, w), lambda bi, i: (bi, i, 0))
    heads = lambda nh: pl.BlockSpec((1, nh, tm, LANE), lambda bi, i: (bi, 0, i, 0))
    tab = pl.BlockSpec((tm, LANE), lambda bi, i: (i, 0))
    out_shape = (
        jax.ShapeDtypeStruct((b, MLA_HEADS, s, LANE), BF16),
        jax.ShapeDtypeStruct((b, MLA_HEADS, s, LANE), BF16),
        jax.ShapeDtypeStruct((b, MLA_HEADS, s, LANE), BF16),
        jax.ShapeDtypeStruct((b, s, SSM_D_INNER), F32),
        jax.ShapeDtypeStruct((b, s, SSM_CONV_CH), F32),
        jax.ShapeDtypeStruct((b, s, LANE), F32),
        jax.ShapeDtypeStruct((b, SWA_HEADS, s, LANE), BF16),
        jax.ShapeDtypeStruct((b, s, LANE), BF16),
        jax.ShapeDtypeStruct((b, s, LANE), BF16),
        jax.ShapeDtypeStruct((b, s, LANE), BF16),
    )
    out_specs = (heads(MLA_HEADS), heads(MLA_HEADS), heads(MLA_HEADS), tok(SSM_D_INNER), tok(SSM_CONV_CH),
                 tok(LANE), heads(SWA_HEADS), tok(LANE), tok(LANE), tok(LANE))
    return pl.pallas_call(
        _in_proj_kernel,
        out_shape=out_shape,
        grid=(b, nt),
        in_specs=[tok(d),
                  pl.BlockSpec((1, 6, d), mod_map),
                  _const_spec((1, d)),
                  _const_spec(p['w_in'].shape),
                  _const_spec((1, MLA_Q_RANK)),
                  _const_spec((1, MLA_KV_RANK)),
                  _const_spec(p['w_uq'].shape),
                  _const_spec(p['w_ukv'].shape),
                  tab, tab, tab, tab],
        out_specs=out_specs,
        compiler_params=_cparams(("parallel", "parallel")),
        name="in_proj",
    )(x, mod, p['norm1_g'], p['w_in'], p['gq'], p['gkv'], p['w_uq'], p['w_ukv'], *tables)


def _lane_fold(x, op):
    r = x[:, :LANE]
    for i in range(1, x.shape[1] // LANE):
        r = op(r, x[:, i * LANE:(i + 1) * LANE])
    return r


def _mla_ctx_kernel(q_ref, kc_ref, vc_ref, o_ref):
    for hp in range(q_ref.shape[1] // 2):
        out = None
        for r in range(2):
            hh = 2 * hp + r
            s = _dot_nt(q_ref[0, hh], kc_ref[0, hh])
            p = jnp.exp(s - jnp.max(s, axis=-1, keepdims=True))
            o = _dot(p.astype(BF16), vc_ref[0, hh]) / jnp.sum(p, axis=-1, keepdims=True)
            out = o if out is None else out + o
        o_ref[0, :, hp * LANE:(hp + 1) * LANE] = out


def _mla_kernel(q_ref, kc_ref, vc_ref, k_ref, v_ref, o_ref, sc_ref, s_ref, mrun_ref, mfin_ref, l_ref, acc_ref, *, tk):
    nh = q_ref.shape[1]
    c = kc_ref.shape[2]
    nk = k_ref.shape[2] // tk

    def scores_ctx(h, slot):
        s = _dot_nt(q_ref[0, h], kc_ref[0, h])
        sc_ref[slot] = s
        mrun_ref[...] = _lane_fold(s, jnp.maximum)

    def scores_tile(h, slot, j):
        off = pl.multiple_of(j * tk, tk)
        s = _dot_nt(q_ref[0, h], k_ref[0, h, pl.ds(off, tk), :])
        s_ref[slot, j] = s
        mrun_ref[...] = jnp.maximum(mrun_ref[...], _lane_fold(s, jnp.maximum))

    def finish_max(slot):
        m = jnp.max(mrun_ref[...], axis=-1, keepdims=True)
        mfin_ref[slot] = jnp.broadcast_to(m, mfin_ref.shape[1:])

    def probs(s, m):
        return jnp.concatenate([jnp.exp(s[:, i * LANE:(i + 1) * LANE] - m) for i in range(s.shape[1] // LANE)], axis=1)

    def values_ctx(h, slot):
        p = probs(sc_ref[slot], mfin_ref[slot])
        l_ref[...] = _lane_fold(p, jnp.add)
        acc_ref[...] = _dot(p.astype(BF16), vc_ref[0, h])

    def values_tile(h, slot, j):
        off = pl.multiple_of(j * tk, tk)
        p = probs(s_ref[slot, j], mfin_ref[slot])
        l_ref[...] += _lane_fold(p, jnp.add)
        acc_ref[...] += _dot(p.astype(BF16), v_ref[0, h, pl.ds(off, tk), :])

    def finish_out(h):
        o = acc_ref[...] / jnp.sum(l_ref[...], axis=-1, keepdims=True)
        blk = (h // 2) * LANE
        if h % 2 == 0:
            o_ref[0, :, blk:blk + LANE] = o
        else:
            o_ref[0, :, blk:blk + LANE] += o

    for h in range(nh + 1):
        slot, prev = h % 2, (h - 1) % 2
        if h < nh:
            scores_ctx(h, slot)
        if h > 0:
            values_ctx(h - 1, prev)

        def body(j, carry, h=h, slot=slot, prev=prev):
            if h < nh:
                scores_tile(h, slot, j)
            if h > 0:
                values_tile(h - 1, prev, j)
            return carry

        lax.fori_loop(0, nk, body, 0, unroll=True)
        if h < nh:
            finish_max(slot)
        if h > 0:
            finish_out(h - 1)


def _mla(q, kc, vc, k, v, tq, tk):
    b, nh, s, _ = q.shape
    c = kc.shape[2]
    allh = lambda n: pl.BlockSpec((1, nh, n, LANE), lambda bi, i: (bi, 0, 0, 0))
    q_spec = pl.BlockSpec((1, nh, tq, LANE), lambda bi, i: (bi, 0, i, 0))
    out_shape = jax.ShapeDtypeStruct((b, s, nh * MLA_V), F32)
    out_spec = pl.BlockSpec((1, tq, nh * MLA_V), lambda bi, i: (bi, i, 0))
    if k is None:
        return pl.pallas_call(
            _mla_ctx_kernel, out_shape=out_shape, grid=(b, s // tq),
            in_specs=[q_spec, allh(c), allh(c)], out_specs=out_spec,
            compiler_params=_cparams(("parallel", "parallel")), name="mla_context",
        )(q, kc, vc)
    sk = k.shape[2]
    return pl.pallas_call(
        functools.partial(_mla_kernel, tk=tk),
        out_shape=out_shape,
        grid=(b, s // tq),
        in_specs=[q_spec, allh(c), allh(c), allh(sk), allh(sk)],
        out_specs=out_spec,
        scratch_shapes=[pltpu.VMEM((2, tq, c), F32), pltpu.VMEM((2, sk // tk, tq, tk), F32), pltpu.VMEM((tq, LANE), F32),
                        pltpu.VMEM((2, tq, LANE), F32), pltpu.VMEM((tq, LANE), F32), pltpu.VMEM((tq, LANE), F32)],
        compiler_params=_cparams(("parallel", "arbitrary")),
        name="mla_latent",
    )(q, kc, vc, k, v)


def _swa_kernel(*refs, has_band, seq):
    if has_band:
        (sink_ref, q_ref, kc_ref, vc_ref, vcr_ref, kp_ref, k0_ref, kn_ref,
         vp_ref, v0_ref, vn_ref, vrp_ref, vr0_ref, vrn_ref, o_ref) = refs
    else:
        sink_ref, q_ref, kc_ref, vc_ref, vcr_ref, o_ref = refs
    tq = q_ref.shape[2]
    i = pl.program_id(1)
    half = lax.broadcasted_iota(jnp.int32, (tq, LANE), 1) // SWA_HEAD_DIM
    kc = kc_ref[0]
    if has_band:
        kb = jnp.concatenate([kp_ref[0], k0_ref[0], kn_ref[0]], axis=0)
        vb = jnp.concatenate([vp_ref[0], v0_ref[0], vn_ref[0]], axis=0)
        vrb = jnp.concatenate([vrp_ref[0], vr0_ref[0], vrn_ref[0]], axis=0)
        qpos = i * tq + lax.broadcasted_iota(jnp.int32, (tq, 3 * tq), 0)
        kpos = (i - 1) * tq + lax.broadcasted_iota(jnp.int32, (tq, 3 * tq), 1)
        valid = (jnp.abs(kpos - qpos) <= WINDOW) & (kpos >= 0) & (kpos < seq)
    for g in range(SWA_KV_HEADS):
        og = None
        for r in range(SWA_HEADS // SWA_KV_HEADS):
            hh = g * (SWA_HEADS // SWA_KV_HEADS) + r
            same = (g == r)
            q = q_ref[0, hh]
            sink = sink_ref[hh]
            sc = _dot_nt(q, kc)
            m = jnp.maximum(jnp.max(sc, axis=-1, keepdims=True), sink)
            if has_band:
                sb = jnp.where(valid, _dot_nt(q, kb), NEG_INF)
                m = jnp.maximum(m, jnp.max(sb, axis=-1, keepdims=True))
            pc = jnp.exp(sc - m)
            l = jnp.sum(pc, axis=-1, keepdims=True) + jnp.exp(sink - m)
            o = _dot(pc.astype(BF16), (vc_ref if same else vcr_ref)[0])
            if has_band:
                pb = jnp.exp(sb - m)
                l = l + jnp.sum(pb, axis=-1, keepdims=True)
                o = o + _dot(pb.astype(BF16), vb if same else vrb)
            o = jnp.where(half == r, o / l, 0.0)
            og = o if og is None else og + o
        o_ref[0, :, g * LANE:(g + 1) * LANE] = og


def _swa(sink, q, kc, vc, vcr, k, v, vr):
    b, nh, s, _ = q.shape
    c = kc.shape[1]
    tq = BLOCK
    nb = s // tq
    has_band = k is not None
    ctx_spec = pl.BlockSpec((1, c, LANE), lambda bi, i: (bi, 0, 0))
    in_specs = [pl.BlockSpec(memory_space=pltpu.SMEM),
                pl.BlockSpec((1, nh, tq, LANE), lambda bi, i: (bi, 0, i, 0)),
                ctx_spec, ctx_spec, ctx_spec]
    args = [sink, q, kc, vc, vcr]
    if has_band:
        prev = pl.BlockSpec((1, tq, LANE), lambda bi, i: (bi, jnp.maximum(i - 1, 0), 0))
        cur = pl.BlockSpec((1, tq, LANE), lambda bi, i: (bi, i, 0))
        nxt = pl.BlockSpec((1, tq, LANE), lambda bi, i: (bi, jnp.minimum(i + 1, nb - 1), 0))
        in_specs += [prev, cur, nxt] * 3
        args += [k, k, k, v, v, v, vr, vr, vr]
    return pl.pallas_call(
        functools.partial(_swa_kernel, has_band=has_band, seq=s),
        out_shape=jax.ShapeDtypeStruct((b, s, nh * SWA_HEAD_DIM), F32),
        grid=(b, nb),
        in_specs=in_specs,
        out_specs=pl.BlockSpec((1, tq, nh * SWA_HEAD_DIM), lambda bi, i: (bi, i, 0)),
        compiler_params=_cparams(("parallel", "parallel")),
        name="swa_latent" if has_band else "swa_context",
    )(*args)


def _split_dot(t_bf16, v):
    hi = v.astype(BF16)
    r1 = v - hi.astype(F32)
    mid = r1.astype(BF16)
    lo = (r1 - mid.astype(F32)).astype(BF16)
    return _dot(t_bf16, hi) + _dot(t_bf16, mid) + _dot(t_bf16, lo)


def _ssd_kernel(xf_ref, xfp_ref, xfn_ref, xb_ref, xbp_ref, xbn_ref, dtf_ref, dtb_ref,
                cw_ref, cb_ref, dtbias_ref, alog_ref, dskip_ref, h0_ref,
                yf_ref, yb_ref, hfin_ref, hst_ref):
    c = pl.program_id(1)
    nc = pl.num_programs(1)
    q = SSM_CHUNK
    gn = SSM_GROUPS * SSM_STATE

    @pl.when(c == 0)
    def _():
        hst_ref[...] = h0_ref[0]

    row = lax.broadcasted_iota(jnp.int32, (q, q), 0)
    col = lax.broadcasted_iota(jnp.int32, (q, q), 1)
    rid = lax.broadcasted_iota(jnp.int32, (q, 1), 0)
    lane = lax.broadcasted_iota(jnp.int32, (1, LANE), 1)
    low_half = lane < SSM_HEAD_DIM
    srow = lax.broadcasted_iota(jnp.int32, (gn, SSM_D_INNER), 0) // SSM_STATE
    scol = lax.broadcasted_iota(jnp.int32, (gn, SSM_D_INNER), 1) // (SSM_D_INNER // SSM_GROUPS)
    gmask = srow == scol
    cw = cw_ref[...]
    a_neg = -jnp.exp(alog_ref[...])

    for d, (x_ref, xp_ref, xn_ref, dt_ref, y_ref) in enumerate(
            ((xf_ref, xfp_ref, xfn_ref, dtf_ref, yf_ref), (xb_ref, xbp_ref, xbn_ref, dtb_ref, yb_ref))):
        cd = c if d == 0 else nc - 1 - c
        keep = (col <= row) if d == 0 else (col >= row)
        x = x_ref[0]
        prev_row = jnp.where(cd > 0, xp_ref[0][SUBLANE - 1:SUBLANE], 0.0)
        next_row = jnp.where(cd < nc - 1, xn_ref[0][0:1], 0.0)
        xm1 = jnp.where(rid == 0, prev_row, pltpu.roll(x, 1, 0))
        xp1 = jnp.where(rid == q - 1, next_row, pltpu.roll(x, q - 1, 0))
        xc = _silu(cw[0:1] * xm1 + cw[1:2] * x + cw[2:3] * xp1 + cb_ref[...])
        xs = xc[:, :SSM_D_INNER]
        bm = xc[:, SSM_D_INNER:SSM_D_INNER + gn]
        cm = xc[:, SSM_D_INNER + gn:]

        dt = _softplus(dt_ref[0] + dtbias_ref[...])
        acs = _split_dot(keep.astype(BF16), dt * a_neg)
        tot = acs[q - 1:q] if d == 0 else acs[0:1]
        acs_t = jnp.transpose(acs)
        e_in = jnp.exp(acs)
        e_out = jnp.exp(tot - acs)
        e_tot = jnp.exp(tot)

        def expand(v, d=d):
            blocks = []
            for bb in range(SSM_HEADS // 2):
                j = d * SSM_HEADS + 2 * bb
                blocks.append(jnp.where(low_half, v[:, j:j + 1], v[:, j + 1:j + 2]))
            return jnp.concatenate(blocks, axis=1)

        xdt = xs * expand(dt)
        cb16 = cm.astype(BF16)
        bb16 = bm.astype(BF16)
        hs = hst_ref[d]
        y_off = _dot(cb16, hs.astype(BF16)) * expand(e_in)
        bt16 = jnp.transpose(bm).astype(BF16)
        states = _dot(bt16, (xdt * expand(e_out)).astype(BF16))
        hst_ref[d] = hs * expand(e_tot) + jnp.where(gmask, states, 0.0)

        cbg = [_dot_nt(cb16[:, g * SSM_STATE:(g + 1) * SSM_STATE], bb16[:, g * SSM_STATE:(g + 1) * SSM_STATE])
               for g in range(SSM_GROUPS)]
        y_blocks = []
        for bb in range(SSM_HEADS // 2):
            xblk = xdt[:, bb * LANE:(bb + 1) * LANE]
            yb = None
            for r in range(2):
                hh = 2 * bb + r
                g = hh // (SSM_HEADS // SSM_GROUPS)
                j = d * SSM_HEADS + hh
                diff = acs[:, j:j + 1] - acs_t[j:j + 1, :]
                decay = jnp.exp(jnp.where(keep, diff, NEG_INF))
                mh = (cbg[g] * decay).astype(BF16)
                xh = jnp.where(low_half if r == 0 else jnp.logical_not(low_half), xblk, 0.0).astype(BF16)
                t = _dot(mh, xh)
                yb = t if yb is None else yb + t
            y_blocks.append(yb)
        y = jnp.concatenate(y_blocks, axis=1) + y_off
        if d == 0:
            y = y + dskip_ref[...] * xs
        y_ref[0] = y

    @pl.when(c == nc - 1)
    def _():
        hfin_ref[0] = hst_ref[...]


def _ssd(xbc, dt, h0, p):
    b, s, _ = xbc.shape
    q = SSM_CHUNK
    nc = s // q
    r8 = q // SUBLANE
    n8 = s // SUBLANE
    gn = SSM_GROUPS * SSM_STATE
    fw = lambda bi, c: (bi, c, 0)
    bw = lambda bi, c: (bi, nc - 1 - c, 0)
    fw_p = lambda bi, c: (bi, jnp.maximum(c * r8 - 1, 0), 0)
    fw_n = lambda bi, c: (bi, jnp.minimum((c + 1) * r8, n8 - 1), 0)
    bw_p = lambda bi, c: (bi, jnp.maximum((nc - 1 - c) * r8 - 1, 0), 0)
    bw_n = lambda bi, c: (bi, jnp.minimum((nc - c) * r8, n8 - 1), 0)
    ch = SSM_CONV_CH
    state_spec = pl.BlockSpec((1, 2, gn, SSM_D_INNER), lambda bi, c: (bi, 0, 0, 0))
    return pl.pallas_call(
        _ssd_kernel,
        out_shape=(jax.ShapeDtypeStruct((b, s, SSM_D_INNER), F32),
                   jax.ShapeDtypeStruct((b, s, SSM_D_INNER), F32),
                   jax.ShapeDtypeStruct((b, 2, gn, SSM_D_INNER), F32)),
        grid=(b, nc),
        in_specs=[pl.BlockSpec((1, q, ch), fw), pl.BlockSpec((1, SUBLANE, ch), fw_p), pl.BlockSpec((1, SUBLANE, ch), fw_n),
                  pl.BlockSpec((1, q, ch), bw), pl.BlockSpec((1, SUBLANE, ch), bw_p), pl.BlockSpec((1, SUBLANE, ch), bw_n),
                  pl.BlockSpec((1, q, LANE), fw), pl.BlockSpec((1, q, LANE), bw),
                  _const_spec((3, ch)), _const_spec((1, ch)), _const_spec((1, LANE)), _const_spec((1, LANE)),
                  _const_spec((1, SSM_D_INNER)), state_spec],
        out_specs=(pl.BlockSpec((1, q, SSM_D_INNER), fw), pl.BlockSpec((1, q, SSM_D_INNER), bw), state_spec),
        scratch_shapes=[pltpu.VMEM((2, gn, SSM_D_INNER), F32)],
        compiler_params=_cparams(("parallel", "arbitrary")),
        name="ssd_scan",
    )(xbc, xbc, xbc, xbc, xbc, xbc, dt, dt, p['conv_w'], p['conv_b'], p['dt_bias'], p['a_log'], p['d_skip'], h0)


def _out_proj_kernel(ya_ref, yf_ref, yb_ref, z_ref, yc_ref, x_ref, mod_ref, gn_ref, w_ref, o_ref):
    y = (yf_ref[0] + yb_ref[0]) * _silu(z_ref[0])
    ybn = _rms(y, gn_ref[...])
    na = ya_ref.shape[2]
    nb = na + ybn.shape[1]
    o = (_dot(ya_ref[0].astype(BF16), w_ref[0:na, :])
         + _dot(ybn.astype(BF16), w_ref[na:nb, :])
         + _dot(yc_ref[0].astype(BF16), w_ref[nb:, :]))
    o_ref[0] = x_ref[0] + mod_ref[0][2:3] * o


def _out_proj(ya, yf, yb, z, yc, x, mod, mod_batched, p, tm):
    b, s, d = x.shape
    mod_map = (lambda bi, i: (bi, 0, 0)) if mod_batched else (lambda bi, i: (0, 0, 0))
    tok = lambda w: pl.BlockSpec((1, tm, w), lambda bi, i: (bi, i, 0))
    return pl.pallas_call(
        _out_proj_kernel,
        out_shape=jax.ShapeDtypeStruct((b, s, d), F32),
        grid=(b, s // tm),
        in_specs=[tok(ya.shape[2]), tok(yf.shape[2]), tok(yb.shape[2]), tok(z.shape[2]), tok(yc.shape[2]), tok(d),
                  pl.BlockSpec((1, 6, d), mod_map), _const_spec((1, SSM_D_INNER)), _const_spec(p['w_out'].shape)],
        out_specs=tok(d),
        compiler_params=_cparams(("parallel", "parallel")),
        name="out_proj",
    )(ya, yf, yb, z, yc, x, mod, p['ssm_norm_g'], p['w_out'])


def _ffn_kernel(*refs, final_norm, tf):
    if final_norm:
        x_ref, xp_ref, xn_ref, mod_ref, g2_ref, wup_ref, cw_ref, cb_ref, wdn_ref, gfin_ref, o_ref, act_ref = refs
    else:
        x_ref, xp_ref, xn_ref, mod_ref, g2_ref, wup_ref, cw_ref, cb_ref, wdn_ref, o_ref, act_ref = refs
    i = pl.program_id(1)
    nt = pl.num_programs(1)
    x = x_ref[0]
    tm = x.shape[0]
    te = tm + 2 * SUBLANE
    f = wdn_ref.shape[0]
    mod = mod_ref[0]
    xe = jnp.concatenate([xp_ref[0], x, xn_ref[0]], axis=0)
    he = (_rms(xe, g2_ref[...]) * (1.0 + mod[4:5]) + mod[3:4]).astype(BF16)
    hc = he[SUBLANE:SUBLANE + tm]
    rid = lax.broadcasted_iota(jnp.int32, (te, 1), 0)
    inside = ((rid >= SUBLANE) | (i > 0)) & ((rid < tm + SUBLANE) | (i < nt - 1))
    for j in range(f // tf):
        val = _dot(hc, wup_ref[:, j * tf:(j + 1) * tf])
        gate = jnp.where(inside, _dot(he, wup_ref[:, f + j * tf:f + (j + 1) * tf]), 0.0)
        cw = cw_ref[:, j * tf:(j + 1) * tf]
        gm1 = pltpu.roll(gate, 1, 0)[SUBLANE:SUBLANE + tm]
        gp1 = pltpu.roll(gate, te - 1, 0)[SUBLANE:SUBLANE + tm]
        gc = cw[0:1] * gm1 + cw[1:2] * gate[SUBLANE:SUBLANE + tm] + cw[2:3] * gp1 + cb_ref[:, j * tf:(j + 1) * tf]
        act_ref[:, j * tf:(j + 1) * tf] = (_silu(gc) * val).astype(BF16)
    out = x + mod[5:6] * _dot(act_ref[...], wdn_ref[...])
    if final_norm:
        out = _rms(out, gfin_ref[...])
    o_ref[0] = out


def _ffn(x, mod, mod_batched, p, gfin, tm):
    b, s, d = x.shape
    f = p['w_down'].shape[0]
    nt = s // tm
    r8 = tm // SUBLANE
    n8 = s // SUBLANE
    mod_map = (lambda bi, i: (bi, 0, 0)) if mod_batched else (lambda bi, i: (0, 0, 0))
    final_norm = gfin is not None
    in_specs = [pl.BlockSpec((1, tm, d), lambda bi, i: (bi, i, 0)),
                pl.BlockSpec((1, SUBLANE, d), lambda bi, i: (bi, jnp.maximum(i * r8 - 1, 0), 0)),
                pl.BlockSpec((1, SUBLANE, d), lambda bi, i: (bi, jnp.minimum((i + 1) * r8, n8 - 1), 0)),
                pl.BlockSpec((1, 6, d), mod_map), _const_spec((1, d)),
                _const_spec(p['w_up'].shape), _const_spec((3, f)), _const_spec((1, f)),
                _const_spec(p['w_down'].shape)]
    args = [x, x, x, mod, p['norm2_g'], p['w_up'], p['ffn_conv_w'], p['ffn_conv_b'], p['w_down']]
    if final_norm:
        in_specs.append(_const_spec((1, d)))
        args.append(gfin)
    return pl.pallas_call(
        functools.partial(_ffn_kernel, final_norm=final_norm, tf=256),
        out_shape=jax.ShapeDtypeStruct((b, s, d), F32),
        grid=(b, nt),
        in_specs=in_specs,
        out_specs=pl.BlockSpec((1, tm, d), lambda bi, i: (bi, i, 0)),
        scratch_shapes=[pltpu.VMEM((tm, f), BF16)],
        compiler_params=_cparams(("parallel", "parallel")),
        name="ffn_final" if final_norm else "ffn",
    )(*args)


def _rope_tables(s):
    t = jnp.arange(s)
    row = (t // GRID_W).astype(F32)[:, None]
    col = (t % GRID_W).astype(F32)[:, None]

    def cs(n):
        inv = jnp.power(ROPE_BASE, -jnp.arange(n, dtype=F32) / n)
        ar, ac = row * inv, col * inv
        cos = jnp.concatenate([jnp.cos(ar), jnp.cos(ar), jnp.cos(ac), jnp.cos(ac)], axis=1)
        sin = jnp.concatenate([-jnp.sin(ar), jnp.sin(ar), -jnp.sin(ac), jnp.sin(ac)], axis=1)
        return cos, sin

    c32, s32 = cs(MLA_ROPE // 4)
    c64, s64 = cs(SWA_HEAD_DIM // 4)
    ones = jnp.ones((s, LANE), F32)
    zeros = jnp.zeros((s, LANE), F32)
    cm = ones.at[:, MLA_NOPE:MLA_NOPE + MLA_ROPE].set(c32)
    sm = zeros.at[:, MLA_NOPE:MLA_NOPE + MLA_ROPE].set(s32)
    return cm, sm, jnp.concatenate([c64, c64], axis=1), jnp.concatenate([s64, s64], axis=1)


def _identity_tables(s):
    ones = jnp.ones((s, LANE), F32)
    zeros = jnp.zeros((s, LANE), F32)
    return ones, zeros, ones, zeros


def _layer_params(l, w):
    d = w['w_in'].shape[1]
    w_in = w['w_in'][l]
    idx = [0]
    for n in IN_SPLITS:
        idx.append(idx[-1] + n)
    qa, kva, kr, z, xbc, dtr, swq, swk, swv = [w_in[:, idx[i]:idx[i + 1]] for i in range(len(IN_SPLITS))]
    zc = lambda n: jnp.zeros((d, n), F32)
    hd = SWA_HEAD_DIM
    swq_blocks = []
    for hh in range(SWA_HEADS):
        qh = swq[:, hh * hd:(hh + 1) * hd]
        swq_blocks += [qh, zc(hd)] if hh < SWA_HEADS // SWA_KV_HEADS else [zc(hd), qh]
    w_in_p = jnp.concatenate(
        [qa, kva, zc(MLA_NOPE), kr, zc(LANE - MLA_NOPE - MLA_ROPE), z, xbc, dtr, zc(LANE - 2 * SSM_HEADS)]
        + swq_blocks + [swk, swv, swv[:, hd:], swv[:, :hd]], axis=1).astype(BF16)

    w_uq = w['mla_w_uq'][l].reshape(MLA_Q_RANK, MLA_HEADS, MLA_NOPE + MLA_ROPE)
    w_uq_p = jnp.pad(w_uq, ((0, 0), (0, 0), (0, LANE - MLA_NOPE - MLA_ROPE))).reshape(MLA_Q_RANK, MLA_HEADS * LANE)
    w_ukv = w['mla_w_ukv'][l].reshape(MLA_KV_RANK, MLA_HEADS, MLA_NOPE + MLA_V)
    wk = jnp.pad(w_ukv[:, :, :MLA_NOPE], ((0, 0), (0, 0), (0, LANE - MLA_NOPE)))
    wv = w_ukv[:, :, MLA_NOPE:]
    zv = jnp.zeros_like(wv)
    even = (jnp.arange(MLA_HEADS) % 2 == 0)[None, :, None]
    wv_p = jnp.concatenate([jnp.where(even, wv, zv), jnp.where(even, zv, wv)], axis=2)
    w_ukv_p = jnp.concatenate([wk.reshape(MLA_KV_RANK, -1), wv_p.reshape(MLA_KV_RANK, -1)], axis=1)

    pad_row = lambda v: jnp.pad(v.reshape(1, -1), ((0, 0), (0, LANE - v.size)))
    return dict(
        norm1_g=w['norm1_g'][l][None], norm2_g=w['norm2_g'][l][None],
        w_in=w_in_p, gq=w['mla_q_norm_g'][l][None], gkv=w['mla_kv_norm_g'][l][None],
        w_uq=w_uq_p.astype(BF16), w_ukv=w_ukv_p.astype(BF16),
        conv_w=w['ssm_conv_w'][l], conv_b=w['ssm_conv_b'][l][None],
        dt_bias=pad_row(w['ssm_dt_bias'][l]), a_log=pad_row(w['ssm_a_log'][l]),
        d_skip=jnp.repeat(w['ssm_d'][l], SSM_HEAD_DIM)[None], ssm_norm_g=w['ssm_norm_g'][l][None],
        sink=w['swa_sink'][l], w_out=w['w_out'][l].astype(BF16),
        w_up=w['ffn_w_up'][l].astype(BF16), ffn_conv_w=w['ffn_conv_w'][l], ffn_conv_b=w['ffn_conv_b'][l][None],
        w_down=w['ffn_w_down'][l].astype(BF16),
    )


def kernel(x, c, ctx, c_ctx, w_mod, b_mod, norm1_g, norm2_g, w_in, mla_q_norm_g, mla_kv_norm_g, mla_w_uq,
           mla_w_ukv, ssm_conv_w, ssm_conv_b, ssm_dt_bias, ssm_a_log, ssm_d, ssm_norm_g, swa_sink, w_out,
           ffn_w_up, ffn_conv_w, ffn_conv_b, ffn_w_down, final_norm_g):
    w = dict(w_in=w_in, norm1_g=norm1_g, norm2_g=norm2_g, mla_q_norm_g=mla_q_norm_g, mla_kv_norm_g=mla_kv_norm_g,
             mla_w_uq=mla_w_uq, mla_w_ukv=mla_w_ukv, ssm_conv_w=ssm_conv_w, ssm_conv_b=ssm_conv_b,
             ssm_dt_bias=ssm_dt_bias, ssm_a_log=ssm_a_log, ssm_d=ssm_d, ssm_norm_g=ssm_norm_g, swa_sink=swa_sink,
             w_out=w_out, ffn_w_up=ffn_w_up, ffn_conv_w=ffn_conv_w, ffn_conv_b=ffn_conv_b, ffn_w_down=ffn_w_down)
    b, s, d = x.shape
    n_ctx = ctx.shape[1]
    depth = w_mod.shape[0]
    tm = min(512, s)
    tmc = min(512, n_ctx)

    rows = 2 * SUBLANE
    cc = jnp.concatenate([c, c_ctx[None], jnp.zeros((rows - b - 1, d), F32)], axis=0)
    mods = _modulation(cc, w_mod, b_mod).reshape(depth, rows, 6, d)

    lat_tables = _rope_tables(s)
    ctx_tables = _identity_tables(n_ctx)
    h_zero = jnp.zeros((b, 2, SSM_GROUPS * SSM_STATE, SSM_D_INNER), F32)

    xc = ctx
    for l in range(depth):
        p = _layer_params(l, w)
        mod = mods[l, :b]
        modc = mods[l, b:b + 1]
        last = l == depth - 1

        qc, kc, vc, zc, xbcc, dtc, swqc, swkc, swvc, swvrc = _in_proj(xc, modc, False, p, ctx_tables, tmc)
        q, k, v, z, xbc, dt, swq, swk, swv, swvr = _in_proj(x, mod, True, p, lat_tables, tm)

        ya = _mla(q, kc, vc, k, v, tq=min(256, s), tk=min(512, s))
        yfc, ybc, h_ctx = _ssd(xbcc, dtc, h_zero, p)
        yf, yb, _ = _ssd(xbc, dt, h_ctx, p)
        yc = _swa(p['sink'], swq, swkc, swvc, swvrc, swk, swv, swvr)

        x = _out_proj(ya, yf, yb, z, yc, x, mod, True, p, tm)
        x = _ffn(x, mod, True, p, final_norm_g[None] if last else None, tm)

        if not last:
            yac = _mla(qc, kc, vc, None, None, tq=min(256, n_ctx), tk=None)
            ycc = _swa(p['sink'], swqc, swkc, swvc, swvrc, None, None, None)
            xc = _out_proj(yac, yfc, ybc, zc, ycc, xc, modc, False, p, tmc)
            xc = _ffn(xc, modc, False, p, None, tmc)
    return x
```

```python
import functools

import jax
import jax.numpy as jnp
from jax import lax
from jax.experimental import pallas as pl
from jax.experimental.pallas import tpu as pltpu

F32 = jnp.float32
BF16 = jnp.bfloat16

NORM_EPS = 1e-6
GRID_W = 64
ROPE_BASE = 10000.0
NEG_INF = -1e30
LOG2_E = 1.4426950408889634
LANE = 128
SUBLANE = 8
VMEM_LIMIT = 56 * 1024 * 1024

MLA_HEADS = 6
MLA_Q_RANK = 256
MLA_KV_RANK = 128
MLA_NOPE = 64
MLA_ROPE = 32
MLA_V = 64
SSM_HEADS = 6
SSM_HEAD_DIM = 64
SSM_D_INNER = SSM_HEADS * SSM_HEAD_DIM
SSM_GROUPS = 2
SSM_STATE = 128
SSM_CHUNK = 128
SSM_CONV_CH = SSM_D_INNER + 2 * SSM_GROUPS * SSM_STATE
SWA_HEADS = 4
SWA_KV_HEADS = 2
SWA_HEAD_DIM = 64
WINDOW = 128
BLOCK = 128
FFN_HIDDEN = 2816

IN_SPLITS = (MLA_Q_RANK, MLA_KV_RANK, MLA_ROPE, SSM_D_INNER, SSM_CONV_CH, 2 * SSM_HEADS,
             SWA_HEADS * SWA_HEAD_DIM, SWA_KV_HEADS * SWA_HEAD_DIM, SWA_KV_HEADS * SWA_HEAD_DIM)

MXU_COLS = 256
O_QA = 0
O_KVA = O_QA + MLA_Q_RANK
O_KR = O_KVA + MLA_KV_RANK
O_Z = O_KR + LANE
O_DT = O_Z + SSM_D_INNER
O_XBC = O_DT + LANE
O_SWK = O_XBC + SSM_CONV_CH
O_SWQ = O_SWK + LANE
O_SWV = O_SWQ + SWA_HEADS * LANE
O_SWVR = O_SWV + LANE
IN_WIDTH_P = O_SWVR + LANE
IN_GROUPS = (O_QA, O_Z, O_XBC, O_SWQ, IN_WIDTH_P)
assert all(g % MXU_COLS == 0 for g in IN_GROUPS)


def _cparams(sem):
    return pltpu.CompilerParams(dimension_semantics=sem, vmem_limit_bytes=VMEM_LIMIT)


def _rms(x, g):
    return x * lax.rsqrt(jnp.mean(x * x, axis=-1, keepdims=True) + NORM_EPS) * g


def _silu(x):
    return x * (1.0 / (1.0 + jnp.exp(-x)))


def _softplus(x):
    return jnp.maximum(x, 0.0) + jnp.log1p(jnp.exp(-jnp.abs(x)))


def _dot(a, b):
    return jnp.dot(a, b, preferred_element_type=F32)


def _dot_nt(a, b):
    return lax.dot_general(a, b, (((1,), (1,)), ((), ())), preferred_element_type=F32)


def _const_spec(shape):
    nd = len(shape)
    return pl.BlockSpec(shape, lambda *_: (0,) * nd)


def _mod_kernel(c_ref, w_ref, b_ref, o_ref):
    s = _silu(c_ref[...]).astype(BF16)
    o_ref[0] = _dot(s, w_ref[0].astype(BF16)) + b_ref[0]


def _modulation(cc, w_mod, b_mod):
    nl, d, n = w_mod.shape
    r = cc.shape[0]
    tn = 1536
    return pl.pallas_call(
        _mod_kernel,
        out_shape=jax.ShapeDtypeStruct((nl, r, n), F32),
        grid=(nl, n // tn),
        in_specs=[pl.BlockSpec((r, d), lambda l, j: (0, 0)),
                  pl.BlockSpec((1, d, tn), lambda l, j: (l, 0, j)),
                  pl.BlockSpec((1, 1, tn), lambda l, j: (l, 0, j))],
        out_specs=pl.BlockSpec((1, r, tn), lambda l, j: (l, 0, j)),
        compiler_params=_cparams(("arbitrary", "arbitrary")),
        name="modulation",
    )(cc, w_mod, b_mod.reshape(nl, 1, n))


def _rope(x, c, s, n, lane):
    first = (lane & (2 * n - 1)) < n
    p = jnp.where(first, pltpu.roll(x, LANE - n, 1), pltpu.roll(x, n, 1))
    return x * c + p * s


def _in_proj_kernel(x_ref, mod_ref, g1_ref, win_ref, gq_ref, gkv_ref, wuq_ref, wukv_ref,
                    cm_ref, sm_ref, cs_ref, ss_ref,
                    q_ref, k_ref, v_ref, z_ref, xbc_ref, dt_ref, swq_ref, swk_ref, swv_ref, swvr_ref):
    x = x_ref[0]
    tm = x.shape[0]
    mod = mod_ref[0]
    h = _rms(x, g1_ref[...]) * (1.0 + mod[1:2]) + mod[0:1]
    hb = h.astype(BF16)

    groups = [_dot(hb, win_ref[:, a:b]) for a, b in zip(IN_GROUPS[:-1], IN_GROUPS[1:])]

    def seg(a, b):
        gi = max(i for i, g in enumerate(IN_GROUPS[:-1]) if g <= a)
        return groups[gi][:, a - IN_GROUPS[gi]:b - IN_GROUPS[gi]]

    lane = lax.broadcasted_iota(jnp.int32, (tm, LANE), 1)
    cm, sm, cs, ss = cm_ref[...], sm_ref[...], cs_ref[...], ss_ref[...]

    qn = _rms(seg(O_QA, O_KVA), gq_ref[...]).astype(BF16)
    qf = _dot(qn, wuq_ref[...])
    q_scale = (MLA_NOPE + MLA_ROPE) ** -0.5 * LOG2_E
    for hh in range(MLA_HEADS):
        qh = _rope(qf[:, hh * LANE:(hh + 1) * LANE], cm, sm, MLA_ROPE // 4, lane)
        q_ref[0, hh] = (qh * q_scale).astype(BF16)
    kn = _rms(seg(O_KVA, O_KR), gkv_ref[...]).astype(BF16)
    kvf = _dot(kn, wukv_ref[...])
    kr = _rope(seg(O_KR, O_Z), cm, sm, MLA_ROPE // 4, lane)
    for hh in range(MLA_HEADS):
        k_ref[0, hh] = (kvf[:, hh * LANE:(hh + 1) * LANE] + kr).astype(BF16)
        v_ref[0, hh] = kvf[:, (MLA_HEADS + hh) * LANE:(MLA_HEADS + hh + 1) * LANE].astype(BF16)

    z_ref[0] = seg(O_Z, O_DT)
    xbc_ref[0] = seg(O_XBC, O_SWK)
    dt_ref[0] = seg(O_DT, O_XBC)

    sw_scale = SWA_HEAD_DIM ** -0.5 * LOG2_E
    for hh in range(SWA_HEADS):
        qh = _rope(seg(O_SWQ + hh * LANE, O_SWQ + (hh + 1) * LANE), cs, ss, SWA_HEAD_DIM // 4, lane)
        swq_ref[0, hh] = (qh * sw_scale).astype(BF16)
    swk_ref[0] = _rope(seg(O_SWK, O_SWQ), cs, ss, SWA_HEAD_DIM // 4, lane).astype(BF16)
    swv_ref[0] = seg(O_SWV, O_SWVR).astype(BF16)
    swvr_ref[0] = seg(O_SWVR, IN_WIDTH_P).astype(BF16)


def _in_proj(x, mod, mod_batched, p, tables, tm):
    b, s, d = x.shape
    nt = s // tm
    mod_map = (lambda bi, i: (bi, 0, 0)) if mod_batched else (lambda bi, i: (0, 0, 0))
    tok = lambda w: pl.BlockSpec((1, tm, w), lambda bi, i: (bi, i, 0))
    heads = lambda nh: pl.BlockSpec((1, nh, tm, LANE), lambda bi, i: (bi, 0, i, 0))
    tab = pl.BlockSpec((tm, LANE), lambda bi, i: (i, 0))
    out_shape = (
        jax.ShapeDtypeStruct((b, MLA_HEADS, s, LANE), BF16),
        jax.ShapeDtypeStruct((b, MLA_HEADS, s, LANE), BF16),
        jax.ShapeDtypeStruct((b, MLA_HEADS, s, LANE), BF16),
        jax.ShapeDtypeStruct((b, s, SSM_D_INNER), F32),
        jax.ShapeDtypeStruct((b, s, SSM_CONV_CH), F32),
        jax.ShapeDtypeStruct((b, s, LANE), F32),
        jax.ShapeDtypeStruct((b, SWA_HEADS, s, LANE), BF16),
        jax.ShapeDtypeStruct((b, s, LANE), BF16),
        jax.ShapeDtypeStruct((b, s, LANE), BF16),
        jax.ShapeDtypeStruct((b, s, LANE), BF16),
    )
    out_specs = (heads(MLA_HEADS), heads(MLA_HEADS), heads(MLA_HEADS), tok(SSM_D_INNER), tok(SSM_CONV_CH),
                 tok(LANE), heads(SWA_HEADS), tok(LANE), tok(LANE), tok(LANE))
    return pl.pallas_call(
        _in_proj_kernel,
        out_shape=out_shape,
        grid=(b, nt),
        in_specs=[tok(d),
                  pl.BlockSpec((1, 6, d), mod_map),
                  _const_spec((1, d)),
                  _const_spec(p['w_in'].shape),
                  _const_spec((1, MLA_Q_RANK)),
                  _const_spec((1, MLA_KV_RANK)),
                  _const_spec(p['w_uq'].shape),
                  _const_spec(p['w_ukv'].shape),
                  tab, tab, tab, tab],
        out_specs=out_specs,
        compiler_params=_cparams(("parallel", "parallel")),
        name="in_proj",
    )(x, mod, p['norm1_g'], p['w_in'], p['gq'], p['gkv'], p['w_uq'], p['w_ukv'], *tables)


def _lane_fold(x, op):
    r = x[:, :LANE]
    for i in range(1, x.shape[1] // LANE):
        r = op(r, x[:, i * LANE:(i + 1) * LANE])
    return r


def _mla_ctx_kernel(q_ref, kc_ref, vc_ref, o_ref):
    for hp in range(q_ref.shape[1] // 2):
        out = None
        for r in range(2):
            hh = 2 * hp + r
            s = _dot_nt(q_ref[0, hh], kc_ref[0, hh])
            p = jnp.exp2(s - jnp.max(s, axis=-1, keepdims=True))
            o = _dot(p.astype(BF16), vc_ref[0, hh]) / jnp.sum(p, axis=-1, keepdims=True)
            out = o if out is None else out + o
        o_ref[0, :, hp * LANE:(hp + 1) * LANE] = out


def _mla_kernel(q_ref, kc_ref, vc_ref, k_ref, v_ref, o_ref, sc_ref, s_ref, mrun_ref, mfin_ref, l_ref, acc_ref, *, tk):
    nh = q_ref.shape[1]
    c = kc_ref.shape[2]
    nk = k_ref.shape[2] // tk

    def scores_ctx(h, slot):
        s = _dot_nt(q_ref[0, h], kc_ref[0, h])
        sc_ref[slot] = s
        mrun_ref[...] = _lane_fold(s, jnp.maximum)

    def scores_tile(h, slot, j):
        off = pl.multiple_of(j * tk, tk)
        s = _dot_nt(q_ref[0, h], k_ref[0, h, pl.ds(off, tk), :])
        s_ref[slot, j] = s
        mrun_ref[...] = jnp.maximum(mrun_ref[...], _lane_fold(s, jnp.maximum))

    def finish_max(slot):
        m = jnp.max(mrun_ref[...], axis=-1, keepdims=True)
        mfin_ref[slot] = jnp.broadcast_to(m, mfin_ref.shape[1:])

    def probs(s, m):
        return jnp.concatenate([jnp.exp2(s[:, i * LANE:(i + 1) * LANE] - m) for i in range(s.shape[1] // LANE)], axis=1)

    def values_ctx(h, slot):
        p = probs(sc_ref[slot], mfin_ref[slot])
        l_ref[...] = _lane_fold(p, jnp.add)
        acc_ref[...] = _dot(p.astype(BF16), vc_ref[0, h])

    def values_tile(h, slot, j):
        off = pl.multiple_of(j * tk, tk)
        p = probs(s_ref[slot, j], mfin_ref[slot])
        l_ref[...] += _lane_fold(p, jnp.add)
        acc_ref[...] += _dot(p.astype(BF16), v_ref[0, h, pl.ds(off, tk), :])

    def finish_out(h):
        o = acc_ref[...] / jnp.sum(l_ref[...], axis=-1, keepdims=True)
        blk = (h // 2) * LANE
        if h % 2 == 0:
            o_ref[0, :, blk:blk + LANE] = o
        else:
            o_ref[0, :, blk:blk + LANE] += o

    for h in range(nh + 1):
        slot, prev = h % 2, (h - 1) % 2
        if h < nh:
            scores_ctx(h, slot)
        if h > 0:
            values_ctx(h - 1, prev)

        def body(j, carry, h=h, slot=slot, prev=prev):
            if h < nh:
                scores_tile(h, slot, j)
            if h > 0:
                values_tile(h - 1, prev, j)
            return carry

        lax.fori_loop(0, nk, body, 0, unroll=True)
        if h < nh:
            finish_max(slot)
        if h > 0:
            finish_out(h - 1)


def _mla(q, kc, vc, k, v, tq, tk):
    b, nh, s, _ = q.shape
    c = kc.shape[2]
    allh = lambda n: pl.BlockSpec((1, nh, n, LANE), lambda bi, i: (bi, 0, 0, 0))
    q_spec = pl.BlockSpec((1, nh, tq, LANE), lambda bi, i: (bi, 0, i, 0))
    out_shape = jax.ShapeDtypeStruct((b, s, nh * MLA_V), F32)
    out_spec = pl.BlockSpec((1, tq, nh * MLA_V), lambda bi, i: (bi, i, 0))
    if k is None:
        return pl.pallas_call(
            _mla_ctx_kernel, out_shape=out_shape, grid=(b, s // tq),
            in_specs=[q_spec, allh(c), allh(c)], out_specs=out_spec,
            compiler_params=_cparams(("parallel", "parallel")), name="mla_context",
        )(q, kc, vc)
    sk = k.shape[2]
    return pl.pallas_call(
        functools.partial(_mla_kernel, tk=tk),
        out_shape=out_shape,
        grid=(b, s // tq),
        in_specs=[q_spec, allh(c), allh(c), allh(sk), allh(sk)],
        out_specs=out_spec,
        scratch_shapes=[pltpu.VMEM((2, tq, c), F32), pltpu.VMEM((2, sk // tk, tq, tk), F32), pltpu.VMEM((tq, LANE), F32),
                        pltpu.VMEM((2, tq, LANE), F32), pltpu.VMEM((tq, LANE), F32), pltpu.VMEM((tq, LANE), F32)],
        compiler_params=_cparams(("parallel", "arbitrary")),
        name="mla_latent",
    )(q, kc, vc, k, v)


def _swa_kernel(*refs, has_band, seq):
    if has_band:
        (sink_ref, q_ref, kc_ref, vc_ref, vcr_ref, kp_ref, k0_ref, kn_ref,
         vp_ref, v0_ref, vn_ref, vrp_ref, vr0_ref, vrn_ref, o_ref) = refs
    else:
        sink_ref, q_ref, kc_ref, vc_ref, vcr_ref, o_ref = refs
    tq = q_ref.shape[2]
    c = kc_ref.shape[1]
    i = pl.program_id(1)
    half = lax.broadcasted_iota(jnp.int32, (tq, LANE), 1) // SWA_HEAD_DIM
    if has_band:
        kall = jnp.concatenate([kc_ref[0], kp_ref[0], k0_ref[0], kn_ref[0]], axis=0)
        vall = jnp.concatenate([vc_ref[0], vp_ref[0], v0_ref[0], vn_ref[0]], axis=0)
        vrall = jnp.concatenate([vcr_ref[0], vrp_ref[0], vr0_ref[0], vrn_ref[0]], axis=0)
        nk = c + tq + 2 * WINDOW
        colk = lax.broadcasted_iota(jnp.int32, (tq, nk), 1)
        qpos = i * tq + lax.broadcasted_iota(jnp.int32, (tq, nk), 0)
        kpos = colk + (i * tq - WINDOW - c)
        valid = (colk < c) | ((jnp.abs(kpos - qpos) <= WINDOW) & (kpos >= 0) & (kpos < seq))
    else:
        kall, vall, vrall = kc_ref[0], vc_ref[0], vcr_ref[0]
    for g in range(SWA_KV_HEADS):
        og = None
        for r in range(SWA_HEADS // SWA_KV_HEADS):
            hh = g * (SWA_HEADS // SWA_KV_HEADS) + r
            sink = sink_ref[hh] * LOG2_E
            s = _dot_nt(q_ref[0, hh], kall)
            if has_band:
                s = jnp.where(valid, s, NEG_INF)
            m = jnp.maximum(jnp.max(s, axis=-1, keepdims=True), sink)
            p = jnp.exp2(s - m)
            l = jnp.sum(p, axis=-1, keepdims=True) + jnp.exp2(sink - m)
            o = _dot(p.astype(BF16), vall if g == r else vrall)
            o = jnp.where(half == r, o / l, 0.0)
            og = o if og is None else og + o
        o_ref[0, :, g * LANE:(g + 1) * LANE] = og


def _swa(sink, q, kc, vc, vcr, k, v, vr, tq):
    b, nh, s, _ = q.shape
    c = kc.shape[1]
    nb = s // tq
    has_band = k is not None
    ctx_spec = pl.BlockSpec((1, c, LANE), lambda bi, i: (bi, 0, 0))
    in_specs = [pl.BlockSpec(memory_space=pltpu.SMEM),
                pl.BlockSpec((1, nh, tq, LANE), lambda bi, i: (bi, 0, i, 0)),
                ctx_spec, ctx_spec, ctx_spec]
    args = [sink, q, kc, vc, vcr]
    if has_band:
        wpt = tq // WINDOW
        nw = s // WINDOW
        prev = pl.BlockSpec((1, WINDOW, LANE), lambda bi, i: (bi, jnp.maximum(i * wpt - 1, 0), 0))
        cur = pl.BlockSpec((1, tq, LANE), lambda bi, i: (bi, i, 0))
        nxt = pl.BlockSpec((1, WINDOW, LANE), lambda bi, i: (bi, jnp.minimum((i + 1) * wpt, nw - 1), 0))
        in_specs += [prev, cur, nxt] * 3
        args += [k, k, k, v, v, v, vr, vr, vr]
    return pl.pallas_call(
        functools.partial(_swa_kernel, has_band=has_band, seq=s),
        out_shape=jax.ShapeDtypeStruct((b, s, nh * SWA_HEAD_DIM), F32),
        grid=(b, nb),
        in_specs=in_specs,
        out_specs=pl.BlockSpec((1, tq, nh * SWA_HEAD_DIM), lambda bi, i: (bi, i, 0)),
        compiler_params=_cparams(("parallel", "parallel")),
        name="swa_latent" if has_band else "swa_context",
    )(*args)


def _split_dot(t_bf16, v):
    hi = v.astype(BF16)
    r1 = v - hi.astype(F32)
    mid = r1.astype(BF16)
    lo = (r1 - mid.astype(F32)).astype(BF16)
    return _dot(t_bf16, hi) + _dot(t_bf16, mid) + _dot(t_bf16, lo)


def _ssd_kernel(xf_ref, xfp_ref, xfn_ref, xb_ref, xbp_ref, xbn_ref, dtf_ref, dtb_ref,
                cw_ref, cb_ref, dtbias_ref, alog_ref, dskip_ref, h0_ref,
                yf_ref, yb_ref, hfin_ref, hst_ref):
    c = pl.program_id(1)
    nc = pl.num_programs(1)
    q = SSM_CHUNK
    gn = SSM_GROUPS * SSM_STATE

    @pl.when(c == 0)
    def _():
        hst_ref[...] = h0_ref[0]

    row = lax.broadcasted_iota(jnp.int32, (q, q), 0)
    col = lax.broadcasted_iota(jnp.int32, (q, q), 1)
    rid = lax.broadcasted_iota(jnp.int32, (q, 1), 0)
    lane = lax.broadcasted_iota(jnp.int32, (1, LANE), 1)
    low_half = lane < SSM_HEAD_DIM
    srow = lax.broadcasted_iota(jnp.int32, (gn, SSM_D_INNER), 0) // SSM_STATE
    scol = lax.broadcasted_iota(jnp.int32, (gn, SSM_D_INNER), 1) // (SSM_D_INNER // SSM_GROUPS)
    gmask = srow == scol
    cw = cw_ref[...]
    a_neg = -jnp.exp(alog_ref[...])

    for d, (x_ref, xp_ref, xn_ref, dt_ref, y_ref) in enumerate(
            ((xf_ref, xfp_ref, xfn_ref, dtf_ref, yf_ref), (xb_ref, xbp_ref, xbn_ref, dtb_ref, yb_ref))):
        cd = c if d == 0 else nc - 1 - c
        keep = (col <= row) if d == 0 else (col >= row)
        x = x_ref[0]
        prev_row = jnp.where(cd > 0, xp_ref[0][SUBLANE - 1:SUBLANE], 0.0)
        next_row = jnp.where(cd < nc - 1, xn_ref[0][0:1], 0.0)
        xm1 = jnp.where(rid == 0, prev_row, pltpu.roll(x, 1, 0))
        xp1 = jnp.where(rid == q - 1, next_row, pltpu.roll(x, q - 1, 0))
        xc = _silu(cw[0:1] * xm1 + cw[1:2] * x + cw[2:3] * xp1 + cb_ref[...])
        xs = xc[:, :SSM_D_INNER]
        bm = xc[:, SSM_D_INNER:SSM_D_INNER + gn]
        cm = xc[:, SSM_D_INNER + gn:]

        dt = _softplus(dt_ref[0] + dtbias_ref[...])
        acs = _split_dot(keep.astype(BF16), dt * a_neg)
        tot = acs[q - 1:q] if d == 0 else acs[0:1]
        acs_t = jnp.transpose(acs)
        e_in = jnp.exp(acs)
        e_out = jnp.exp(tot - acs)
        e_tot = jnp.exp(tot)

        def expand(v, d=d):
            blocks = []
            for bb in range(SSM_HEADS // 2):
                j = d * SSM_HEADS + 2 * bb
                blocks.append(jnp.where(low_half, v[:, j:j + 1], v[:, j + 1:j + 2]))
            return jnp.concatenate(blocks, axis=1)

        xdt = xs * expand(dt)
        cb16 = cm.astype(BF16)
        bb16 = bm.astype(BF16)
        hs = hst_ref[d]
        y_off = _dot(cb16, hs.astype(BF16)) * expand(e_in)
        bt16 = jnp.transpose(bm).astype(BF16)
        states = _dot(bt16, (xdt * expand(e_out)).astype(BF16))
        hst_ref[d] = hs * expand(e_tot) + jnp.where(gmask, states, 0.0)

        cbg = [_dot_nt(cb16[:, g * SSM_STATE:(g + 1) * SSM_STATE], bb16[:, g * SSM_STATE:(g + 1) * SSM_STATE])
               for g in range(SSM_GROUPS)]
        y_blocks = []
        for bb in range(SSM_HEADS // 2):
            xblk = xdt[:, bb * LANE:(bb + 1) * LANE]
            yb = None
            for r in range(2):
                hh = 2 * bb + r
                g = hh // (SSM_HEADS // SSM_GROUPS)
                j = d * SSM_HEADS + hh
                diff = acs[:, j:j + 1] - acs_t[j:j + 1, :]
                decay = jnp.exp(jnp.where(keep, diff, NEG_INF))
                mh = (cbg[g] * decay).astype(BF16)
                xh = jnp.where(low_half if r == 0 else jnp.logical_not(low_half), xblk, 0.0).astype(BF16)
                t = _dot(mh, xh)
                yb = t if yb is None else yb + t
            y_blocks.append(yb)
        y = jnp.concatenate(y_blocks, axis=1) + y_off
        if d == 0:
            y = y + dskip_ref[...] * xs
        y_ref[0] = y

    @pl.when(c == nc - 1)
    def _():
        hfin_ref[0] = hst_ref[...]


def _ssd(xbc, dt, h0, p):
    b, s, _ = xbc.shape
    q = SSM_CHUNK
    nc = s // q
    r8 = q // SUBLANE
    n8 = s // SUBLANE
    gn = SSM_GROUPS * SSM_STATE
    fw = lambda bi, c: (bi, c, 0)
    bw = lambda bi, c: (bi, nc - 1 - c, 0)
    fw_p = lambda bi, c: (bi, jnp.maximum(c * r8 - 1, 0), 0)
    fw_n = lambda bi, c: (bi, jnp.minimum((c + 1) * r8, n8 - 1), 0)
    bw_p = lambda bi, c: (bi, jnp.maximum((nc - 1 - c) * r8 - 1, 0), 0)
    bw_n = lambda bi, c: (bi, jnp.minimum((nc - c) * r8, n8 - 1), 0)
    ch = SSM_CONV_CH
    state_spec = pl.BlockSpec((1, 2, gn, SSM_D_INNER), lambda bi, c: (bi, 0, 0, 0))
    return pl.pallas_call(
        _ssd_kernel,
        out_shape=(jax.ShapeDtypeStruct((b, s, SSM_D_INNER), F32),
                   jax.ShapeDtypeStruct((b, s, SSM_D_INNER), F32),
                   jax.ShapeDtypeStruct((b, 2, gn, SSM_D_INNER), F32)),
        grid=(b, nc),
        in_specs=[pl.BlockSpec((1, q, ch), fw), pl.BlockSpec((1, SUBLANE, ch), fw_p), pl.BlockSpec((1, SUBLANE, ch), fw_n),
                  pl.BlockSpec((1, q, ch), bw), pl.BlockSpec((1, SUBLANE, ch), bw_p), pl.BlockSpec((1, SUBLANE, ch), bw_n),
                  pl.BlockSpec((1, q, LANE), fw), pl.BlockSpec((1, q, LANE), bw),
                  _const_spec((3, ch)), _const_spec((1, ch)), _const_spec((1, LANE)), _const_spec((1, LANE)),
                  _const_spec((1, SSM_D_INNER)), state_spec],
        out_specs=(pl.BlockSpec((1, q, SSM_D_INNER), fw), pl.BlockSpec((1, q, SSM_D_INNER), bw), state_spec),
        scratch_shapes=[pltpu.VMEM((2, gn, SSM_D_INNER), F32)],
        compiler_params=_cparams(("parallel", "arbitrary")),
        name="ssd_scan",
    )(xbc, xbc, xbc, xbc, xbc, xbc, dt, dt, p['conv_w'], p['conv_b'], p['dt_bias'], p['a_log'], p['d_skip'], h0)


def _out_proj_kernel(ya_ref, yf_ref, yb_ref, z_ref, yc_ref, x_ref, mod_ref, gn_ref, w_ref, o_ref):
    y = (yf_ref[0] + yb_ref[0]) * _silu(z_ref[0])
    ybn = _rms(y, gn_ref[...])
    na = ya_ref.shape[2]
    nb = na + ybn.shape[1]
    o = (_dot(ya_ref[0].astype(BF16), w_ref[0:na, :])
         + _dot(ybn.astype(BF16), w_ref[na:nb, :])
         + _dot(yc_ref[0].astype(BF16), w_ref[nb:, :]))
    o_ref[0] = x_ref[0] + mod_ref[0][2:3] * o


def _out_proj(ya, yf, yb, z, yc, x, mod, mod_batched, p, tm):
    b, s, d = x.shape
    mod_map = (lambda bi, i: (bi, 0, 0)) if mod_batched else (lambda bi, i: (0, 0, 0))
    tok = lambda w: pl.BlockSpec((1, tm, w), lambda bi, i: (bi, i, 0))
    return pl.pallas_call(
        _out_proj_kernel,
        out_shape=jax.ShapeDtypeStruct((b, s, d), F32),
        grid=(b, s // tm),
        in_specs=[tok(ya.shape[2]), tok(yf.shape[2]), tok(yb.shape[2]), tok(z.shape[2]), tok(yc.shape[2]), tok(d),
                  pl.BlockSpec((1, 6, d), mod_map), _const_spec((1, SSM_D_INNER)), _const_spec(p['w_out'].shape)],
        out_specs=tok(d),
        compiler_params=_cparams(("parallel", "parallel")),
        name="out_proj",
    )(ya, yf, yb, z, yc, x, mod, p['ssm_norm_g'], p['w_out'])


def _ffn_kernel(*refs, final_norm, tf):
    if final_norm:
        x_ref, xp_ref, xn_ref, mod_ref, g2_ref, wup_ref, cw_ref, cb_ref, wdn_ref, gfin_ref, o_ref, act_ref = refs
    else:
        x_ref, xp_ref, xn_ref, mod_ref, g2_ref, wup_ref, cw_ref, cb_ref, wdn_ref, o_ref, act_ref = refs
    i = pl.program_id(1)
    nt = pl.num_programs(1)
    x = x_ref[0]
    tm = x.shape[0]
    te = tm + 2 * SUBLANE
    f = wdn_ref.shape[0]
    mod = mod_ref[0]
    xe = jnp.concatenate([xp_ref[0], x, xn_ref[0]], axis=0)
    he = (_rms(xe, g2_ref[...]) * (1.0 + mod[4:5]) + mod[3:4]).astype(BF16)
    hc = he[SUBLANE:SUBLANE + tm]
    rid = lax.broadcasted_iota(jnp.int32, (te, 1), 0)
    inside = ((rid >= SUBLANE) | (i > 0)) & ((rid < tm + SUBLANE) | (i < nt - 1))
    for j in range(f // tf):
        val = _dot(hc, wup_ref[:, j * tf:(j + 1) * tf])
        gate = jnp.where(inside, _dot(he, wup_ref[:, f + j * tf:f + (j + 1) * tf]), 0.0)
        cw = cw_ref[:, j * tf:(j + 1) * tf]
        gm1 = pltpu.roll(gate, 1, 0)[SUBLANE:SUBLANE + tm]
        gp1 = pltpu.roll(gate, te - 1, 0)[SUBLANE:SUBLANE + tm]
        gc = cw[0:1] * gm1 + cw[1:2] * gate[SUBLANE:SUBLANE + tm] + cw[2:3] * gp1 + cb_ref[:, j * tf:(j + 1) * tf]
        act_ref[:, j * tf:(j + 1) * tf] = (_silu(gc) * val).astype(BF16)
    out = x + mod[5:6] * _dot(act_ref[...], wdn_ref[...])
    if final_norm:
        out = _rms(out, gfin_ref[...])
    o_ref[0] = out


def _ffn(x, mod, mod_batched, p, gfin, tm):
    b, s, d = x.shape
    f = p['w_down'].shape[0]
    nt = s // tm
    r8 = tm // SUBLANE
    n8 = s // SUBLANE
    mod_map = (lambda bi, i: (bi, 0, 0)) if mod_batched else (lambda bi, i: (0, 0, 0))
    final_norm = gfin is not None
    in_specs = [pl.BlockSpec((1, tm, d), lambda bi, i: (bi, i, 0)),
                pl.BlockSpec((1, SUBLANE, d), lambda bi, i: (bi, jnp.maximum(i * r8 - 1, 0), 0)),
                pl.BlockSpec((1, SUBLANE, d), lambda bi, i: (bi, jnp.minimum((i + 1) * r8, n8 - 1), 0)),
                pl.BlockSpec((1, 6, d), mod_map), _const_spec((1, d)),
                _const_spec(p['w_up'].shape), _const_spec((3, f)), _const_spec((1, f)),
                _const_spec(p['w_down'].shape)]
    args = [x, x, x, mod, p['norm2_g'], p['w_up'], p['ffn_conv_w'], p['ffn_conv_b'], p['w_down']]
    if final_norm:
        in_specs.append(_const_spec((1, d)))
        args.append(gfin)
    return pl.pallas_call(
        functools.partial(_ffn_kernel, final_norm=final_norm, tf=256),
        out_shape=jax.ShapeDtypeStruct((b, s, d), F32),
        grid=(b, nt),
        in_specs=in_specs,
        out_specs=pl.BlockSpec((1, tm, d), lambda bi, i: (bi, i, 0)),
        scratch_shapes=[pltpu.VMEM((tm, f), BF16)],
        compiler_params=_cparams(("parallel", "parallel")),
        name="ffn_final" if final_norm else "ffn",
    )(*args)


def _rope_tables(s):
    t = jnp.arange(s)
    row = (t // GRID_W).astype(F32)[:, None]
    col = (t % GRID_W).astype(F32)[:, None]

    def cs(n):
        inv = jnp.power(ROPE_BASE, -jnp.arange(n, dtype=F32) / n)
        ar, ac = row * inv, col * inv
        cos = jnp.concatenate([jnp.cos(ar), jnp.cos(ar), jnp.cos(ac), jnp.cos(ac)], axis=1)
        sin = jnp.concatenate([-jnp.sin(ar), jnp.sin(ar), -jnp.sin(ac), jnp.sin(ac)], axis=1)
        return cos, sin

    c32, s32 = cs(MLA_ROPE // 4)
    c64, s64 = cs(SWA_HEAD_DIM // 4)
    ones = jnp.ones((s, LANE), F32)
    zeros = jnp.zeros((s, LANE), F32)
    cm = ones.at[:, MLA_NOPE:MLA_NOPE + MLA_ROPE].set(c32)
    sm = zeros.at[:, MLA_NOPE:MLA_NOPE + MLA_ROPE].set(s32)
    return cm, sm, jnp.concatenate([c64, c64], axis=1), jnp.concatenate([s64, s64], axis=1)


def _identity_tables(s):
    ones = jnp.ones((s, LANE), F32)
    zeros = jnp.zeros((s, LANE), F32)
    return ones, zeros, ones, zeros


def _layer_params(l, w):
    d = w['w_in'].shape[1]
    w_in = w['w_in'][l]
    idx = [0]
    for n in IN_SPLITS:
        idx.append(idx[-1] + n)
    qa, kva, kr, z, xbc, dtr, swq, swk, swv = [w_in[:, idx[i]:idx[i + 1]] for i in range(len(IN_SPLITS))]
    zc = lambda n: jnp.zeros((d, n), F32)
    hd = SWA_HEAD_DIM
    swq_blocks = []
    for hh in range(SWA_HEADS):
        qh = swq[:, hh * hd:(hh + 1) * hd]
        swq_blocks += [qh, zc(hd)] if hh < SWA_HEADS // SWA_KV_HEADS else [zc(hd), qh]
    w_in_p = jnp.concatenate(
        [qa, kva, zc(MLA_NOPE), kr, zc(LANE - MLA_NOPE - MLA_ROPE), z, dtr, zc(LANE - 2 * SSM_HEADS), xbc, swk]
        + swq_blocks + [swv, swv[:, hd:], swv[:, :hd]], axis=1).astype(BF16)

    w_uq = w['mla_w_uq'][l].reshape(MLA_Q_RANK, MLA_HEADS, MLA_NOPE + MLA_ROPE)
    w_uq_p = jnp.pad(w_uq, ((0, 0), (0, 0), (0, LANE - MLA_NOPE - MLA_ROPE))).reshape(MLA_Q_RANK, MLA_HEADS * LANE)
    w_ukv = w['mla_w_ukv'][l].reshape(MLA_KV_RANK, MLA_HEADS, MLA_NOPE + MLA_V)
    wk = jnp.pad(w_ukv[:, :, :MLA_NOPE], ((0, 0), (0, 0), (0, LANE - MLA_NOPE)))
    wv = w_ukv[:, :, MLA_NOPE:]
    zv = jnp.zeros_like(wv)
    even = (jnp.arange(MLA_HEADS) % 2 == 0)[None, :, None]
    wv_p = jnp.concatenate([jnp.where(even, wv, zv), jnp.where(even, zv, wv)], axis=2)
    w_ukv_p = jnp.concatenate([wk.reshape(MLA_KV_RANK, -1), wv_p.reshape(MLA_KV_RANK, -1)], axis=1)

    pad_row = lambda v: jnp.pad(v.reshape(1, -1), ((0, 0), (0, LANE - v.size)))
    return dict(
        norm1_g=w['norm1_g'][l][None], norm2_g=w['norm2_g'][l][None],
        w_in=w_in_p, gq=w['mla_q_norm_g'][l][None], gkv=w['mla_kv_norm_g'][l][None],
        w_uq=w_uq_p.astype(BF16), w_ukv=w_ukv_p.astype(BF16),
        conv_w=w['ssm_conv_w'][l], conv_b=w['ssm_conv_b'][l][None],
        dt_bias=pad_row(w['ssm_dt_bias'][l]), a_log=pad_row(w['ssm_a_log'][l]),
        d_skip=jnp.repeat(w['ssm_d'][l], SSM_HEAD_DIM)[None], ssm_norm_g=w['ssm_norm_g'][l][None],
        sink=w['swa_sink'][l], w_out=w['w_out'][l].astype(BF16),
        w_up=w['ffn_w_up'][l].astype(BF16), ffn_conv_w=w['ffn_conv_w'][l], ffn_conv_b=w['ffn_conv_b'][l][None],
        w_down=w['ffn_w_down'][l].astype(BF16),
    )


def kernel(x, c, ctx, c_ctx, w_mod, b_mod, norm1_g, norm2_g, w_in, mla_q_norm_g, mla_kv_norm_g, mla_w_uq,
           mla_w_ukv, ssm_conv_w, ssm_conv_b, ssm_dt_bias, ssm_a_log, ssm_d, ssm_norm_g, swa_sink, w_out,
           ffn_w_up, ffn_conv_w, ffn_conv_b, ffn_w_down, final_norm_g):
    w = dict(w_in=w_in, norm1_g=norm1_g, norm2_g=norm2_g, mla_q_norm_g=mla_q_norm_g, mla_kv_norm_g=mla_kv_norm_g,
             mla_w_uq=mla_w_uq, mla_w_ukv=mla_w_ukv, ssm_conv_w=ssm_conv_w, ssm_conv_b=ssm_conv_b,
             ssm_dt_bias=ssm_dt_bias, ssm_a_log=ssm_a_log, ssm_d=ssm_d, ssm_norm_g=ssm_norm_g, swa_sink=swa_sink,
             w_out=w_out, ffn_w_up=ffn_w_up, ffn_conv_w=ffn_conv_w, ffn_conv_b=ffn_conv_b, ffn_w_down=ffn_w_down)
    b, s, d = x.shape
    n_ctx = ctx.shape[1]
    depth = w_mod.shape[0]
    tm = min(512, s)
    tmc = min(512, n_ctx)

    rows = 2 * SUBLANE
    cc = jnp.concatenate([c, c_ctx[None], jnp.zeros((rows - b - 1, d), F32)], axis=0)
    mods = _modulation(cc, w_mod, b_mod).reshape(depth, rows, 6, d)

    lat_tables = _rope_tables(s)
    ctx_tables = _identity_tables(n_ctx)
    h_zero = jnp.zeros((b, 2, SSM_GROUPS * SSM_STATE, SSM_D_INNER), F32)

    xc = ctx
    for l in range(depth):
        p = _layer_params(l, w)
        mod = mods[l, :b]
        modc = mods[l, b:b + 1]
        last = l == depth - 1

        qc, kc, vc, zc, xbcc, dtc, swqc, swkc, swvc, swvrc = _in_proj(xc, modc, False, p, ctx_tables, tmc)
        q, k, v, z, xbc, dt, swq, swk, swv, swvr = _in_proj(x, mod, True, p, lat_tables, tm)

        ya = _mla(q, kc, vc, k, v, tq=min(256, s), tk=min(1024, s))
        yfc, ybc, h_ctx = _ssd(xbcc, dtc, h_zero, p)
        yf, yb, _ = _ssd(xbc, dt, h_ctx, p)
        yc = _swa(p['sink'], swq, swkc, swvc, swvrc, swk, swv, swvr, tq=min(256, s))

        x = _out_proj(ya, yf, yb, z, yc, x, mod, True, p, tm)
        x = _ffn(x, mod, True, p, final_norm_g[None] if last else None, tm)

        if not last:
            yac = _mla(qc, kc, vc, None, None, tq=min(256, n_ctx), tk=None)
            ycc = _swa(p['sink'], swqc, swkc, swvc, swvrc, None, None, None, tq=min(256, n_ctx))
            xc = _out_proj(yac, yfc, ybc, zc, ycc, xc, modc, False, p, tmc)
            xc = _ffn(xc, modc, False, p, None, tmc)
    return x
```

```python
import functools

import jax
import jax.numpy as jnp
from jax import lax
from jax.experimental import pallas as pl
from jax.experimental.pallas import tpu as pltpu

F32 = jnp.float32
BF16 = jnp.bfloat16

NORM_EPS = 1e-6
GRID_W = 64
ROPE_BASE = 10000.0
NEG_INF = -1e30
LOG2_E = 1.4426950408889634
LANE = 128
SUBLANE = 8
VMEM_LIMIT = 56 * 1024 * 1024

MLA_HEADS = 6
MLA_Q_RANK = 256
MLA_KV_RANK = 128
MLA_NOPE = 64
MLA_ROPE = 32
MLA_V = 64
SSM_HEADS = 6
SSM_HEAD_DIM = 64
SSM_D_INNER = SSM_HEADS * SSM_HEAD_DIM
SSM_GROUPS = 2
SSM_STATE = 128
SSM_CHUNK = 128
SSM_CONV_CH = SSM_D_INNER + 2 * SSM_GROUPS * SSM_STATE
SWA_HEADS = 4
SWA_KV_HEADS = 2
SWA_HEAD_DIM = 64
WINDOW = 128
BLOCK = 128
FFN_HIDDEN = 2816

IN_SPLITS = (MLA_Q_RANK, MLA_KV_RANK, MLA_ROPE, SSM_D_INNER, SSM_CONV_CH, 2 * SSM_HEADS,
             SWA_HEADS * SWA_HEAD_DIM, SWA_KV_HEADS * SWA_HEAD_DIM, SWA_KV_HEADS * SWA_HEAD_DIM)

MXU_COLS = 256
O_QA = 0
O_KVA = O_QA + MLA_Q_RANK
O_KR = O_KVA + MLA_KV_RANK
O_Z = O_KR + LANE
O_DT = O_Z + SSM_D_INNER
O_XBC = O_DT + LANE
O_SWK = O_XBC + SSM_CONV_CH
O_SWQ = O_SWK + LANE
O_SWV = O_SWQ + SWA_HEADS * LANE
O_SWVR = O_SWV + LANE
IN_WIDTH_P = O_SWVR + LANE
IN_GROUPS = (O_QA, O_Z, O_XBC, O_SWQ, IN_WIDTH_P)
assert all(g % MXU_COLS == 0 for g in IN_GROUPS)


def _cparams(sem):
    return pltpu.CompilerParams(dimension_semantics=sem, vmem_limit_bytes=VMEM_LIMIT)


def _rms(x, g):
    return x * lax.rsqrt(jnp.mean(x * x, axis=-1, keepdims=True) + NORM_EPS) * g


def _silu(x):
    return x * (1.0 / (1.0 + jnp.exp(-x)))


def _softplus(x):
    return jnp.maximum(x, 0.0) + jnp.log1p(jnp.exp(-jnp.abs(x)))


def _dot(a, b):
    return jnp.dot(a, b, preferred_element_type=F32)


def _dot_nt(a, b):
    return lax.dot_general(a, b, (((1,), (1,)), ((), ())), preferred_element_type=F32)


def _const_spec(shape):
    nd = len(shape)
    return pl.BlockSpec(shape, lambda *_: (0,) * nd)


def _mod_kernel(c_ref, w_ref, b_ref, o_ref):
    s = _silu(c_ref[...]).astype(BF16)
    o_ref[0] = _dot(s, w_ref[0].astype(BF16)) + b_ref[0]


def _modulation(cc, w_mod, b_mod):
    nl, d, n = w_mod.shape
    r = cc.shape[0]
    tn = 1536
    return pl.pallas_call(
        _mod_kernel,
        out_shape=jax.ShapeDtypeStruct((nl, r, n), F32),
        grid=(nl, n // tn),
        in_specs=[pl.BlockSpec((r, d), lambda l, j: (0, 0)),
                  pl.BlockSpec((1, d, tn), lambda l, j: (l, 0, j)),
                  pl.BlockSpec((1, 1, tn), lambda l, j: (l, 0, j))],
        out_specs=pl.BlockSpec((1, r, tn), lambda l, j: (l, 0, j)),
        compiler_params=_cparams(("arbitrary", "arbitrary")),
        name="modulation",
    )(cc, w_mod, b_mod.reshape(nl, 1, n))


def _rope(x, c, s, n, lane):
    first = (lane & (2 * n - 1)) < n
    p = jnp.where(first, pltpu.roll(x, LANE - n, 1), pltpu.roll(x, n, 1))
    return x * c + p * s


def _in_proj_kernel(x_ref, xp_ref, xn_ref, mod_ref, g1_ref, win_ref, gq_ref, gkv_ref, wuq_ref, wukv_ref,
                    cw_ref, cb_ref, cm_ref, sm_ref, cs_ref, ss_ref,
                    q_ref, k_ref, v_ref, z_ref, xbc_ref, dt_ref, swq_ref, swk_ref, swv_ref, swvr_ref):
    i = pl.program_id(1)
    nt = pl.num_programs(1)
    tm = x_ref.shape[1]
    te = tm + 2 * SUBLANE
    mod = mod_ref[0]
    xe = jnp.concatenate([xp_ref[0], x_ref[0], xn_ref[0]], axis=0)
    he = (_rms(xe, g1_ref[...]) * (1.0 + mod[1:2]) + mod[0:1]).astype(BF16)
    hb = he[SUBLANE:SUBLANE + tm]

    conv_group = IN_GROUPS.index(O_XBC)
    groups = [_dot(he if gi == conv_group else hb, win_ref[:, a:b])
              for gi, (a, b) in enumerate(zip(IN_GROUPS[:-1], IN_GROUPS[1:]))]

    def seg(a, b, halo=False):
        gi = max(i for i, g in enumerate(IN_GROUPS[:-1]) if g <= a)
        g = groups[gi][:, a - IN_GROUPS[gi]:b - IN_GROUPS[gi]]
        return g if halo or gi != conv_group else g[SUBLANE:SUBLANE + tm]

    lane = lax.broadcasted_iota(jnp.int32, (tm, LANE), 1)
    cm, sm, cs, ss = cm_ref[...], sm_ref[...], cs_ref[...], ss_ref[...]

    qn = _rms(seg(O_QA, O_KVA), gq_ref[...]).astype(BF16)
    qf = _dot(qn, wuq_ref[...])
    q_scale = (MLA_NOPE + MLA_ROPE) ** -0.5 * LOG2_E
    for hh in range(MLA_HEADS):
        qh = _rope(qf[:, hh * LANE:(hh + 1) * LANE], cm, sm, MLA_ROPE // 4, lane)
        q_ref[0, hh] = (qh * q_scale).astype(BF16)
    kn = _rms(seg(O_KVA, O_KR), gkv_ref[...]).astype(BF16)
    kvf = _dot(kn, wukv_ref[...])
    kr = _rope(seg(O_KR, O_Z), cm, sm, MLA_ROPE // 4, lane)
    for hh in range(MLA_HEADS):
        k_ref[0, hh] = (kvf[:, hh * LANE:(hh + 1) * LANE] + kr).astype(BF16)
        v_ref[0, hh] = kvf[:, (MLA_HEADS + hh) * LANE:(MLA_HEADS + hh + 1) * LANE].astype(BF16)

    z_ref[0] = seg(O_Z, O_DT)
    dt_ref[0] = seg(O_DT, O_XBC)
    rid = lax.broadcasted_iota(jnp.int32, (te, 1), 0)
    inside = ((rid >= SUBLANE) | (i > 0)) & ((rid < tm + SUBLANE) | (i < nt - 1))
    xbc = jnp.where(inside, seg(O_XBC, O_SWK, halo=True), 0.0)
    cw = cw_ref[...]
    xm1 = pltpu.roll(xbc, 1, 0)[SUBLANE:SUBLANE + tm]
    xp1 = pltpu.roll(xbc, te - 1, 0)[SUBLANE:SUBLANE + tm]
    xbc_ref[0] = _silu(cw[0:1] * xm1 + cw[1:2] * xbc[SUBLANE:SUBLANE + tm] + cw[2:3] * xp1 + cb_ref[...])

    sw_scale = SWA_HEAD_DIM ** -0.5 * LOG2_E
    for hh in range(SWA_HEADS):
        qh = _rope(seg(O_SWQ + hh * LANE, O_SWQ + (hh + 1) * LANE), cs, ss, SWA_HEAD_DIM // 4, lane)
        swq_ref[0, hh] = (qh * sw_scale).astype(BF16)
    swk_ref[0] = _rope(seg(O_SWK, O_SWQ), cs, ss, SWA_HEAD_DIM // 4, lane).astype(BF16)
    swv_ref[0] = seg(O_SWV, O_SWVR).astype(BF16)
    swvr_ref[0] = seg(O_SWVR, IN_WIDTH_P).astype(BF16)


def _in_proj(x, mod, mod_batched, p, tables, tm):
    b, s, d = x.shape
    nt = s // tm
    mod_map = (lambda bi, i: (bi, 0, 0)) if mod_batched else (lambda bi, i: (0, 0, 0))
    tok = lambda w: pl.BlockSpec((1, tm, w), lambda bi, i: (bi, i, 0))
    heads = lambda nh: pl.BlockSpec((1, nh, tm, LANE), lambda bi, i: (bi, 0, i, 0))
    tab = pl.BlockSpec((tm, LANE), lambda bi, i: (i, 0))
    out_shape = (
        jax.ShapeDtypeStruct((b, MLA_HEADS, s, LANE), BF16),
        jax.ShapeDtypeStruct((b, MLA_HEADS, s, LANE), BF16),
        jax.ShapeDtypeStruct((b, MLA_HEADS, s, LANE), BF16),
        jax.ShapeDtypeStruct((b, s, SSM_D_INNER), F32),
        jax.ShapeDtypeStruct((b, s, SSM_CONV_CH), F32),
        jax.ShapeDtypeStruct((b, s, LANE), F32),
        jax.ShapeDtypeStruct((b, SWA_HEADS, s, LANE), BF16),
        jax.ShapeDtypeStruct((b, s, LANE), BF16),
        jax.ShapeDtypeStruct((b, s, LANE), BF16),
        jax.ShapeDtypeStruct((b, s, LANE), BF16),
    )
    out_specs = (heads(MLA_HEADS), heads(MLA_HEADS), heads(MLA_HEADS), tok(SSM_D_INNER), tok(SSM_CONV_CH),
                 tok(LANE), heads(SWA_HEADS), tok(LANE), tok(LANE), tok(LANE))
    r8 = tm // SUBLANE
    n8 = s // SUBLANE
    return pl.pallas_call(
        _in_proj_kernel,
        out_shape=out_shape,
        grid=(b, nt),
        in_specs=[tok(d),
                  pl.BlockSpec((1, SUBLANE, d), lambda bi, i: (bi, jnp.maximum(i * r8 - 1, 0), 0)),
                  pl.BlockSpec((1, SUBLANE, d), lambda bi, i: (bi, jnp.minimum((i + 1) * r8, n8 - 1), 0)),
                  pl.BlockSpec((1, 6, d), mod_map),
                  _const_spec((1, d)),
                  _const_spec(p['w_in'].shape),
                  _const_spec((1, MLA_Q_RANK)),
                  _const_spec((1, MLA_KV_RANK)),
                  _const_spec(p['w_uq'].shape),
                  _const_spec(p['w_ukv'].shape),
                  _const_spec((3, SSM_CONV_CH)), _const_spec((1, SSM_CONV_CH)),
                  tab, tab, tab, tab],
        out_specs=out_specs,
        compiler_params=_cparams(("parallel", "parallel")),
        name="in_proj",
    )(x, x, x, mod, p['norm1_g'], p['w_in'], p['gq'], p['gkv'], p['w_uq'], p['w_ukv'],
      p['conv_w'], p['conv_b'], *tables)


def _lane_fold(x, op):
    r = x[:, :LANE]
    for i in range(1, x.shape[1] // LANE):
        r = op(r, x[:, i * LANE:(i + 1) * LANE])
    return r


def _mla_ctx_kernel(q_ref, kc_ref, vc_ref, o_ref):
    for hp in range(q_ref.shape[1] // 2):
        out = None
        for r in range(2):
            hh = 2 * hp + r
            s = _dot_nt(q_ref[0, hh], kc_ref[0, hh])
            p = jnp.exp2(s - jnp.max(s, axis=-1, keepdims=True))
            o = _dot(p.astype(BF16), vc_ref[0, hh]) / jnp.sum(p, axis=-1, keepdims=True)
            out = o if out is None else out + o
        o_ref[0, :, hp * LANE:(hp + 1) * LANE] = out


def _mla_kernel(q_ref, kc_ref, vc_ref, k_ref, v_ref, o_ref, sc_ref, s_ref, mrun_ref, mfin_ref, l_ref, acc_ref, *, tk):
    nh = q_ref.shape[1]
    c = kc_ref.shape[2]
    nk = k_ref.shape[2] // tk

    def scores_ctx(h, slot):
        s = _dot_nt(q_ref[0, h], kc_ref[0, h])
        sc_ref[slot] = s
        mrun_ref[...] = _lane_fold(s, jnp.maximum)

    def scores_tile(h, slot, j):
        off = pl.multiple_of(j * tk, tk)
        s = _dot_nt(q_ref[0, h], k_ref[0, h, pl.ds(off, tk), :])
        s_ref[slot, j] = s
        mrun_ref[...] = jnp.maximum(mrun_ref[...], _lane_fold(s, jnp.maximum))

    def finish_max(slot):
        m = jnp.max(mrun_ref[...], axis=-1, keepdims=True)
        mfin_ref[slot] = jnp.broadcast_to(m, mfin_ref.shape[1:])

    def probs(s, m):
        return jnp.concatenate([jnp.exp2(s[:, i * LANE:(i + 1) * LANE] - m) for i in range(s.shape[1] // LANE)], axis=1)

    def values_ctx(h, slot):
        p = probs(sc_ref[slot], mfin_ref[slot])
        l_ref[...] = _lane_fold(p, jnp.add)
        acc_ref[...] = _dot(p.astype(BF16), vc_ref[0, h])

    def values_tile(h, slot, j):
        off = pl.multiple_of(j * tk, tk)
        p = probs(s_ref[slot, j], mfin_ref[slot])
        l_ref[...] += _lane_fold(p, jnp.add)
        acc_ref[...] += _dot(p.astype(BF16), v_ref[0, h, pl.ds(off, tk), :])

    def finish_out(h):
        o = acc_ref[...] / jnp.sum(l_ref[...], axis=-1, keepdims=True)
        blk = (h // 2) * LANE
        if h % 2 == 0:
            o_ref[0, :, blk:blk + LANE] = o
        else:
            o_ref[0, :, blk:blk + LANE] += o

    for h in range(nh + 1):
        slot, prev = h % 2, (h - 1) % 2
        if h < nh:
            scores_ctx(h, slot)
        if h > 0:
            values_ctx(h - 1, prev)

        def body(j, carry, h=h, slot=slot, prev=prev):
            if h < nh:
                scores_tile(h, slot, j)
            if h > 0:
                values_tile(h - 1, prev, j)
            return carry

        lax.fori_loop(0, nk, body, 0, unroll=True)
        if h < nh:
            finish_max(slot)
        if h > 0:
            finish_out(h - 1)


def _mla(q, kc, vc, k, v, tq, tk):
    b, nh, s, _ = q.shape
    c = kc.shape[2]
    allh = lambda n: pl.BlockSpec((1, nh, n, LANE), lambda bi, i: (bi, 0, 0, 0))
    q_spec = pl.BlockSpec((1, nh, tq, LANE), lambda bi, i: (bi, 0, i, 0))
    out_shape = jax.ShapeDtypeStruct((b, s, nh * MLA_V), F32)
    out_spec = pl.BlockSpec((1, tq, nh * MLA_V), lambda bi, i: (bi, i, 0))
    if k is None:
        return pl.pallas_call(
            _mla_ctx_kernel, out_shape=out_shape, grid=(b, s // tq),
            in_specs=[q_spec, allh(c), allh(c)], out_specs=out_spec,
            compiler_params=_cparams(("parallel", "parallel")), name="mla_context",
        )(q, kc, vc)
    sk = k.shape[2]
    return pl.pallas_call(
        functools.partial(_mla_kernel, tk=tk),
        out_shape=out_shape,
        grid=(b, s // tq),
        in_specs=[q_spec, allh(c), allh(c), allh(sk), allh(sk)],
        out_specs=out_spec,
        scratch_shapes=[pltpu.VMEM((2, tq, c), F32), pltpu.VMEM((2, sk // tk, tq, tk), F32), pltpu.VMEM((tq, LANE), F32),
                        pltpu.VMEM((2, tq, LANE), F32), pltpu.VMEM((tq, LANE), F32), pltpu.VMEM((tq, LANE), F32)],
        compiler_params=_cparams(("parallel", "arbitrary")),
        name="mla_latent",
    )(q, kc, vc, k, v)


def _swa_kernel(*refs, has_band, seq):
    if has_band:
        (sink_ref, q_ref, kc_ref, vc_ref, vcr_ref, kp_ref, k0_ref, kn_ref,
         vp_ref, v0_ref, vn_ref, vrp_ref, vr0_ref, vrn_ref, o_ref) = refs
    else:
        sink_ref, q_ref, kc_ref, vc_ref, vcr_ref, o_ref = refs
    tq = q_ref.shape[2]
    c = kc_ref.shape[1]
    i = pl.program_id(1)
    half = lax.broadcasted_iota(jnp.int32, (tq, LANE), 1) // SWA_HEAD_DIM
    if has_band:
        kall = jnp.concatenate([kc_ref[0], kp_ref[0], k0_ref[0], kn_ref[0]], axis=0)
        vall = jnp.concatenate([vc_ref[0], vp_ref[0], v0_ref[0], vn_ref[0]], axis=0)
        vrall = jnp.concatenate([vcr_ref[0], vrp_ref[0], vr0_ref[0], vrn_ref[0]], axis=0)
        nk = c + tq + 2 * WINDOW
        colk = lax.broadcasted_iota(jnp.int32, (tq, nk), 1)
        qpos = i * tq + lax.broadcasted_iota(jnp.int32, (tq, nk), 0)
        kpos = colk + (i * tq - WINDOW - c)
        valid = (colk < c) | ((jnp.abs(kpos - qpos) <= WINDOW) & (kpos >= 0) & (kpos < seq))
    else:
        kall, vall, vrall = kc_ref[0], vc_ref[0], vcr_ref[0]
    rep = SWA_HEADS // SWA_KV_HEADS
    heads = range(SWA_HEADS)
    s_all = [_dot_nt(q_ref[0, hh], kall) for hh in heads]
    p_all, l_all = [], []
    for hh in heads:
        sink = sink_ref[hh] * LOG2_E
        s = jnp.where(valid, s_all[hh], NEG_INF) if has_band else s_all[hh]
        m = jnp.maximum(jnp.max(s, axis=-1, keepdims=True), sink)
        p = jnp.exp2(s - m)
        l_all.append(jnp.sum(p, axis=-1, keepdims=True) + jnp.exp2(sink - m))
        p_all.append(p.astype(BF16))
    o_all = [_dot(p_all[hh], vall if hh // rep == hh % rep else vrall) for hh in heads]
    for g in range(SWA_KV_HEADS):
        og = None
        for r in range(rep):
            hh = g * rep + r
            o = jnp.where(half == r, o_all[hh] / l_all[hh], 0.0)
            og = o if og is None else og + o
        o_ref[0, :, g * LANE:(g + 1) * LANE] = og


def _swa(sink, q, kc, vc, vcr, k, v, vr, tq):
    b, nh, s, _ = q.shape
    c = kc.shape[1]
    nb = s // tq
    has_band = k is not None
    ctx_spec = pl.BlockSpec((1, c, LANE), lambda bi, i: (bi, 0, 0))
    in_specs = [pl.BlockSpec(memory_space=pltpu.SMEM),
                pl.BlockSpec((1, nh, tq, LANE), lambda bi, i: (bi, 0, i, 0)),
                ctx_spec, ctx_spec, ctx_spec]
    args = [sink, q, kc, vc, vcr]
    if has_band:
        wpt = tq // WINDOW
        nw = s // WINDOW
        prev = pl.BlockSpec((1, WINDOW, LANE), lambda bi, i: (bi, jnp.maximum(i * wpt - 1, 0), 0))
        cur = pl.BlockSpec((1, tq, LANE), lambda bi, i: (bi, i, 0))
        nxt = pl.BlockSpec((1, WINDOW, LANE), lambda bi, i: (bi, jnp.minimum((i + 1) * wpt, nw - 1), 0))
        in_specs += [prev, cur, nxt] * 3
        args += [k, k, k, v, v, v, vr, vr, vr]
    return pl.pallas_call(
        functools.partial(_swa_kernel, has_band=has_band, seq=s),
        out_shape=jax.ShapeDtypeStruct((b, s, nh * SWA_HEAD_DIM), F32),
        grid=(b, nb),
        in_specs=in_specs,
        out_specs=pl.BlockSpec((1, tq, nh * SWA_HEAD_DIM), lambda bi, i: (bi, i, 0)),
        compiler_params=_cparams(("parallel", "parallel")),
        name="swa_latent" if has_band else "swa_context",
    )(*args)


def _split_dot(t_bf16, v):
    hi = v.astype(BF16)
    r1 = v - hi.astype(F32)
    mid = r1.astype(BF16)
    lo = (r1 - mid.astype(F32)).astype(BF16)
    return _dot(t_bf16, hi) + _dot(t_bf16, mid) + _dot(t_bf16, lo)


def _split_lhs_dot(v, t_bf16, pieces):
    out, rest = None, v
    for i in range(pieces):
        part = rest.astype(BF16)
        if i + 1 < pieces:
            rest = rest - part.astype(F32)
        term = _dot(part, t_bf16)
        out = term if out is None else out + term
    return out


def _ssd_kernel(xf_ref, xb_ref, dtf_ref, dtb_ref, dtbias_ref, alog_ref, dskip_ref, ex_ref, h0_ref,
                yf_ref, yb_ref, hfin_ref, hst_ref):
    c = pl.program_id(1)
    nc = pl.num_programs(1)
    q = SSM_CHUNK
    gn = SSM_GROUPS * SSM_STATE

    @pl.when(c == 0)
    def _():
        hst_ref[...] = h0_ref[...]

    row = lax.broadcasted_iota(jnp.int32, (q, q), 0)
    col = lax.broadcasted_iota(jnp.int32, (q, q), 1)
    lane = lax.broadcasted_iota(jnp.int32, (1, LANE), 1)
    low_half = lane < SSM_HEAD_DIM
    srow = lax.broadcasted_iota(jnp.int32, (gn, SSM_D_INNER), 0) // SSM_STATE
    scol = lax.broadcasted_iota(jnp.int32, (gn, SSM_D_INNER), 1) // (SSM_D_INNER // SSM_GROUPS)
    gmask = srow == scol
    a_neg = -jnp.exp(alog_ref[...])

    dirs = ((xf_ref, dtf_ref, yf_ref), (xb_ref, dtb_ref, yb_ref))
    keeps = ((col <= row), (col >= row))
    chains = [dict(bi=bi, d=d) for bi in range(hst_ref.shape[0]) for d in range(2)]

    for ch in chains:
        bi, d = ch['bi'], ch['d']
        xc = dirs[d][0][bi]
        ch['xs'] = xc[:, :SSM_D_INNER]
        ch['bm'] = xc[:, SSM_D_INNER:SSM_D_INNER + gn]
        ch['cm'] = xc[:, SSM_D_INNER + gn:]
        ch['dt'] = _softplus(dirs[d][1][bi] + dtbias_ref[...])
        ch['acs'] = _split_dot(keeps[d].astype(BF16), ch['dt'] * a_neg)

    for ch in chains:
        d = ch['d']
        ch['acs_x'] = _split_lhs_dot(ch['acs'], ex_ref[d], 3)
        ch['dt_x'] = _split_lhs_dot(ch['dt'], ex_ref[d], 2)
        ch['acs_t'] = jnp.transpose(ch['acs'])
        ch['cb16'] = ch['cm'].astype(BF16)
        ch['bb16'] = ch['bm'].astype(BF16)
        ch['bt16'] = jnp.transpose(ch['bm']).astype(BF16)
        ch['cbg'] = [_dot_nt(ch['cb16'][:, g * SSM_STATE:(g + 1) * SSM_STATE],
                             ch['bb16'][:, g * SSM_STATE:(g + 1) * SSM_STATE]) for g in range(SSM_GROUPS)]

    for ch in chains:
        bi, d = ch['bi'], ch['d']
        acs_x = ch['acs_x']
        tot_x = acs_x[q - 1:q] if d == 0 else acs_x[0:1]
        ch['xdt'] = ch['xs'] * ch['dt_x']
        hs = hst_ref[bi, d]
        ch['y_off'] = _dot(ch['cb16'], hs.astype(BF16)) * jnp.exp(acs_x)
        states = _dot(ch['bt16'], (ch['xdt'] * jnp.exp(tot_x - acs_x)).astype(BF16))
        hst_ref[bi, d] = hs * jnp.exp(tot_x) + jnp.where(gmask, states, 0.0)

    for bb in range(SSM_HEADS // 2):
        for ch in chains:
            d = ch['d']
            xblk = ch['xdt'][:, bb * LANE:(bb + 1) * LANE]
            ablk = ch['acs_x'][:, bb * LANE:(bb + 1) * LANE]
            aswap = pltpu.roll(ablk, SSM_HEAD_DIM, 1)
            yb = None
            for r in range(2):
                hh = 2 * bb + r
                g = hh // (SSM_HEADS // SSM_GROUPS)
                j = d * SSM_HEADS + hh
                acol = jnp.where(low_half, ablk, aswap) if r == 0 else jnp.where(low_half, aswap, ablk)
                diff = acol - ch['acs_t'][j:j + 1, :]
                decay = jnp.exp(jnp.where(keeps[d], diff, NEG_INF))
                mh = (ch['cbg'][g] * decay).astype(BF16)
                xh = jnp.where(low_half if r == 0 else jnp.logical_not(low_half), xblk, 0.0).astype(BF16)
                t = _dot(mh, xh)
                yb = t if yb is None else yb + t
            ch.setdefault('y_blocks', []).append(yb)

    for ch in chains:
        bi, d = ch['bi'], ch['d']
        y = jnp.concatenate(ch['y_blocks'], axis=1) + ch['y_off']
        if d == 0:
            y = y + dskip_ref[...] * ch['xs']
        dirs[d][2][bi] = y

    @pl.when(c == nc - 1)
    def _():
        hfin_ref[...] = hst_ref[...]


def _head_expander():
    rows = jnp.arange(LANE)[None, :, None]
    heads = (jnp.arange(SSM_D_INNER) // SSM_HEAD_DIM)[None, None, :]
    dirs = jnp.arange(2)[:, None, None]
    return (rows == dirs * SSM_HEADS + heads).astype(BF16)


def _ssd(xbc, dt, h0, p, nbb):
    b, s, _ = xbc.shape
    q = SSM_CHUNK
    nc = s // q
    gn = SSM_GROUPS * SSM_STATE
    fw = lambda bi, c: (bi, c, 0)
    bw = lambda bi, c: (bi, nc - 1 - c, 0)
    ch = SSM_CONV_CH
    state_spec = pl.BlockSpec((nbb, 2, gn, SSM_D_INNER), lambda bi, c: (bi, 0, 0, 0))
    return pl.pallas_call(
        _ssd_kernel,
        out_shape=(jax.ShapeDtypeStruct((b, s, SSM_D_INNER), F32),
                   jax.ShapeDtypeStruct((b, s, SSM_D_INNER), F32),
                   jax.ShapeDtypeStruct((b, 2, gn, SSM_D_INNER), F32)),
        grid=(b // nbb, nc),
        in_specs=[pl.BlockSpec((nbb, q, ch), fw), pl.BlockSpec((nbb, q, ch), bw),
                  pl.BlockSpec((nbb, q, LANE), fw), pl.BlockSpec((nbb, q, LANE), bw),
                  _const_spec((1, LANE)), _const_spec((1, LANE)), _const_spec((1, SSM_D_INNER)),
                  _const_spec((2, LANE, SSM_D_INNER)), state_spec],
        out_specs=(pl.BlockSpec((nbb, q, SSM_D_INNER), fw), pl.BlockSpec((nbb, q, SSM_D_INNER), bw), state_spec),
        scratch_shapes=[pltpu.VMEM((nbb, 2, gn, SSM_D_INNER), F32)],
        compiler_params=_cparams(("parallel", "arbitrary")),
        name="ssd_scan",
    )(xbc, xbc, dt, dt, p['dt_bias'], p['a_log'], p['d_skip'], _head_expander(), h0)


def _out_proj_kernel(ya_ref, yf_ref, yb_ref, z_ref, yc_ref, x_ref, mod_ref, gn_ref, w_ref, o_ref):
    y = (yf_ref[0] + yb_ref[0]) * _silu(z_ref[0])
    ybn = _rms(y, gn_ref[...])
    na = ya_ref.shape[2]
    nb = na + ybn.shape[1]
    o = (_dot(ya_ref[0].astype(BF16), w_ref[0:na, :])
         + _dot(ybn.astype(BF16), w_ref[na:nb, :])
         + _dot(yc_ref[0].astype(BF16), w_ref[nb:, :]))
    o_ref[0] = x_ref[0] + mod_ref[0][2:3] * o


def _out_proj(ya, yf, yb, z, yc, x, mod, mod_batched, p, tm):
    b, s, d = x.shape
    mod_map = (lambda bi, i: (bi, 0, 0)) if mod_batched else (lambda bi, i: (0, 0, 0))
    tok = lambda w: pl.BlockSpec((1, tm, w), lambda bi, i: (bi, i, 0))
    return pl.pallas_call(
        _out_proj_kernel,
        out_shape=jax.ShapeDtypeStruct((b, s, d), F32),
        grid=(b, s // tm),
        in_specs=[tok(ya.shape[2]), tok(yf.shape[2]), tok(yb.shape[2]), tok(z.shape[2]), tok(yc.shape[2]), tok(d),
                  pl.BlockSpec((1, 6, d), mod_map), _const_spec((1, SSM_D_INNER)), _const_spec(p['w_out'].shape)],
        out_specs=tok(d),
        compiler_params=_cparams(("parallel", "parallel")),
        name="out_proj",
    )(ya, yf, yb, z, yc, x, mod, p['ssm_norm_g'], p['w_out'])


def _ffn_kernel(*refs, final_norm, tf):
    if final_norm:
        x_ref, xp_ref, xn_ref, mod_ref, g2_ref, wup_ref, cw_ref, cb_ref, wdn_ref, gfin_ref, o_ref, act_ref = refs
    else:
        x_ref, xp_ref, xn_ref, mod_ref, g2_ref, wup_ref, cw_ref, cb_ref, wdn_ref, o_ref, act_ref = refs
    i = pl.program_id(1)
    nt = pl.num_programs(1)
    x = x_ref[0]
    tm = x.shape[0]
    te = tm + 2 * SUBLANE
    f = wdn_ref.shape[0]
    mod = mod_ref[0]
    xe = jnp.concatenate([xp_ref[0], x, xn_ref[0]], axis=0)
    he = (_rms(xe, g2_ref[...]) * (1.0 + mod[4:5]) + mod[3:4]).astype(BF16)
    hc = he[SUBLANE:SUBLANE + tm]
    rid = lax.broadcasted_iota(jnp.int32, (te, 1), 0)
    inside = ((rid >= SUBLANE) | (i > 0)) & ((rid < tm + SUBLANE) | (i < nt - 1))
    for j in range(f // tf):
        val = _dot(hc, wup_ref[:, j * tf:(j + 1) * tf])
        gate = jnp.where(inside, _dot(he, wup_ref[:, f + j * tf:f + (j + 1) * tf]), 0.0)
        cw = cw_ref[:, j * tf:(j + 1) * tf]
        gm1 = pltpu.roll(gate, 1, 0)[SUBLANE:SUBLANE + tm]
        gp1 = pltpu.roll(gate, te - 1, 0)[SUBLANE:SUBLANE + tm]
        gc = cw[0:1] * gm1 + cw[1:2] * gate[SUBLANE:SUBLANE + tm] + cw[2:3] * gp1 + cb_ref[:, j * tf:(j + 1) * tf]
        act_ref[:, j * tf:(j + 1) * tf] = (_silu(gc) * val).astype(BF16)
    out = x + mod[5:6] * _dot(act_ref[...], wdn_ref[...])
    if final_norm:
        out = _rms(out, gfin_ref[...])
    o_ref[0] = out


def _ffn(x, mod, mod_batched, p, gfin, tm):
    b, s, d = x.shape
    f = p['w_down'].shape[0]
    nt = s // tm
    r8 = tm // SUBLANE
    n8 = s // SUBLANE
    mod_map = (lambda bi, i: (bi, 0, 0)) if mod_batched else (lambda bi, i: (0, 0, 0))
    final_norm = gfin is not None
    in_specs = [pl.BlockSpec((1, tm, d), lambda bi, i: (bi, i, 0)),
                pl.BlockSpec((1, SUBLANE, d), lambda bi, i: (bi, jnp.maximum(i * r8 - 1, 0), 0)),
                pl.BlockSpec((1, SUBLANE, d), lambda bi, i: (bi, jnp.minimum((i + 1) * r8, n8 - 1), 0)),
                pl.BlockSpec((1, 6, d), mod_map), _const_spec((1, d)),
                _const_spec(p['w_up'].shape), _const_spec((3, f)), _const_spec((1, f)),
                _const_spec(p['w_down'].shape)]
    args = [x, x, x, mod, p['norm2_g'], p['w_up'], p['ffn_conv_w'], p['ffn_conv_b'], p['w_down']]
    if final_norm:
        in_specs.append(_const_spec((1, d)))
        args.append(gfin)
    return pl.pallas_call(
        functools.partial(_ffn_kernel, final_norm=final_norm, tf=256),
        out_shape=jax.ShapeDtypeStruct((b, s, d), F32),
        grid=(b, nt),
        in_specs=in_specs,
        out_specs=pl.BlockSpec((1, tm, d), lambda bi, i: (bi, i, 0)),
        scratch_shapes=[pltpu.VMEM((tm, f), BF16)],
        compiler_params=_cparams(("parallel", "parallel")),
        name="ffn_final" if final_norm else "ffn",
    )(*args)


def _rope_tables(s):
    t = jnp.arange(s)
    row = (t // GRID_W).astype(F32)[:, None]
    col = (t % GRID_W).astype(F32)[:, None]

    def cs(n):
        inv = jnp.power(ROPE_BASE, -jnp.arange(n, dtype=F32) / n)
        ar, ac = row * inv, col * inv
        cos = jnp.concatenate([jnp.cos(ar), jnp.cos(ar), jnp.cos(ac), jnp.cos(ac)], axis=1)
        sin = jnp.concatenate([-jnp.sin(ar), jnp.sin(ar), -jnp.sin(ac), jnp.sin(ac)], axis=1)
        return cos, sin

    c32, s32 = cs(MLA_ROPE // 4)
    c64, s64 = cs(SWA_HEAD_DIM // 4)
    ones = jnp.ones((s, LANE), F32)
    zeros = jnp.zeros((s, LANE), F32)
    cm = ones.at[:, MLA_NOPE:MLA_NOPE + MLA_ROPE].set(c32)
    sm = zeros.at[:, MLA_NOPE:MLA_NOPE + MLA_ROPE].set(s32)
    return cm, sm, jnp.concatenate([c64, c64], axis=1), jnp.concatenate([s64, s64], axis=1)


def _identity_tables(s):
    ones = jnp.ones((s, LANE), F32)
    zeros = jnp.zeros((s, LANE), F32)
    return ones, zeros, ones, zeros


def _layer_params(l, w):
    d = w['w_in'].shape[1]
    w_in = w['w_in'][l]
    idx = [0]
    for n in IN_SPLITS:
        idx.append(idx[-1] + n)
    qa, kva, kr, z, xbc, dtr, swq, swk, swv = [w_in[:, idx[i]:idx[i + 1]] for i in range(len(IN_SPLITS))]
    zc = lambda n: jnp.zeros((d, n), F32)
    hd = SWA_HEAD_DIM
    swq_blocks = []
    for hh in range(SWA_HEADS):
        qh = swq[:, hh * hd:(hh + 1) * hd]
        swq_blocks += [qh, zc(hd)] if hh < SWA_HEADS // SWA_KV_HEADS else [zc(hd), qh]
    w_in_p = jnp.concatenate(
        [qa, kva, zc(MLA_NOPE), kr, zc(LANE - MLA_NOPE - MLA_ROPE), z, dtr, zc(LANE - 2 * SSM_HEADS), xbc, swk]
        + swq_blocks + [swv, swv[:, hd:], swv[:, :hd]], axis=1).astype(BF16)

    w_uq = w['mla_w_uq'][l].reshape(MLA_Q_RANK, MLA_HEADS, MLA_NOPE + MLA_ROPE)
    w_uq_p = jnp.pad(w_uq, ((0, 0), (0, 0), (0, LANE - MLA_NOPE - MLA_ROPE))).reshape(MLA_Q_RANK, MLA_HEADS * LANE)
    w_ukv = w['mla_w_ukv'][l].reshape(MLA_KV_RANK, MLA_HEADS, MLA_NOPE + MLA_V)
    wk = jnp.pad(w_ukv[:, :, :MLA_NOPE], ((0, 0), (0, 0), (0, LANE - MLA_NOPE)))
    wv = w_ukv[:, :, MLA_NOPE:]
    zv = jnp.zeros_like(wv)
    even = (jnp.arange(MLA_HEADS) % 2 == 0)[None, :, None]
    wv_p = jnp.concatenate([jnp.where(even, wv, zv), jnp.where(even, zv, wv)], axis=2)
    w_ukv_p = jnp.concatenate([wk.reshape(MLA_KV_RANK, -1), wv_p.reshape(MLA_KV_RANK, -1)], axis=1)

    pad_row = lambda v: jnp.pad(v.reshape(1, -1), ((0, 0), (0, LANE - v.size)))
    return dict(
        norm1_g=w['norm1_g'][l][None], norm2_g=w['norm2_g'][l][None],
        w_in=w_in_p, gq=w['mla_q_norm_g'][l][None], gkv=w['mla_kv_norm_g'][l][None],
        w_uq=w_uq_p.astype(BF16), w_ukv=w_ukv_p.astype(BF16),
        conv_w=w['ssm_conv_w'][l], conv_b=w['ssm_conv_b'][l][None],
        dt_bias=pad_row(w['ssm_dt_bias'][l]), a_log=pad_row(w['ssm_a_log'][l]),
        d_skip=jnp.repeat(w['ssm_d'][l], SSM_HEAD_DIM)[None], ssm_norm_g=w['ssm_norm_g'][l][None],
        sink=w['swa_sink'][l], w_out=w['w_out'][l].astype(BF16),
        w_up=w['ffn_w_up'][l].astype(BF16), ffn_conv_w=w['ffn_conv_w'][l], ffn_conv_b=w['ffn_conv_b'][l][None],
        w_down=w['ffn_w_down'][l].astype(BF16),
    )


def kernel(x, c, ctx, c_ctx, w_mod, b_mod, norm1_g, norm2_g, w_in, mla_q_norm_g, mla_kv_norm_g, mla_w_uq,
           mla_w_ukv, ssm_conv_w, ssm_conv_b, ssm_dt_bias, ssm_a_log, ssm_d, ssm_norm_g, swa_sink, w_out,
           ffn_w_up, ffn_conv_w, ffn_conv_b, ffn_w_down, final_norm_g):
    w = dict(w_in=w_in, norm1_g=norm1_g, norm2_g=norm2_g, mla_q_norm_g=mla_q_norm_g, mla_kv_norm_g=mla_kv_norm_g,
             mla_w_uq=mla_w_uq, mla_w_ukv=mla_w_ukv, ssm_conv_w=ssm_conv_w, ssm_conv_b=ssm_conv_b,
             ssm_dt_bias=ssm_dt_bias, ssm_a_log=ssm_a_log, ssm_d=ssm_d, ssm_norm_g=ssm_norm_g, swa_sink=swa_sink,
             w_out=w_out, ffn_w_up=ffn_w_up, ffn_conv_w=ffn_conv_w, ffn_conv_b=ffn_conv_b, ffn_w_down=ffn_w_down)
    b, s, d = x.shape
    n_ctx = ctx.shape[1]
    depth = w_mod.shape[0]
    tm = min(512, s)
    tmc = min(512, n_ctx)

    rows = 2 * SUBLANE
    cc = jnp.concatenate([c, c_ctx[None], jnp.zeros((rows - b - 1, d), F32)], axis=0)
    mods = _modulation(cc, w_mod, b_mod).reshape(depth, rows, 6, d)

    lat_tables = _rope_tables(s)
    ctx_tables = _identity_tables(n_ctx)
    h_zero = jnp.zeros((b, 2, SSM_GROUPS * SSM_STATE, SSM_D_INNER), F32)

    xc = ctx
    for l in range(depth):
        p = _layer_params(l, w)
        mod = mods[l, :b]
        modc = mods[l, b:b + 1]
        last = l == depth - 1

        qc, kc, vc, zc, xbcc, dtc, swqc, swkc, swvc, swvrc = _in_proj(xc, modc, False, p, ctx_tables, tmc)
        q, k, v, z, xbc, dt, swq, swk, swv, swvr = _in_proj(x, mod, True, p, lat_tables, tm)

        ya = _mla(q, kc, vc, k, v, tq=min(256, s), tk=min(1024, s))
        nbb = 2 if b % 2 == 0 else 1
        yfc, ybc, h_ctx = _ssd(xbcc, dtc, h_zero, p, nbb)
        yf, yb, _ = _ssd(xbc, dt, h_ctx, p, nbb)
        yc = _swa(p['sink'], swq, swkc, swvc, swvrc, swk, swv, swvr, tq=min(256, s))

        x = _out_proj(ya, yf, yb, z, yc, x, mod, True, p, tm)
        x = _ffn(x, mod, True, p, final_norm_g[None] if last else None, tm)

        if not last:
            yac = _mla(qc, kc, vc, None, None, tq=min(256, n_ctx), tk=None)
            ycc = _swa(p['sink'], swqc, swkc, swvc, swvrc, None, None, None, tq=min(256, n_ctx))
            xc = _out_proj(yac, yfc, ybc, zc, ycc, xc, modc, False, p, tmc)
            xc = _ffn(xc, modc, False, p, None, tmc)
    return x
```

```python
import functools

import jax
import jax.numpy as jnp
from jax import lax
from jax.experimental import pallas as pl
from jax.experimental.pallas import tpu as pltpu

F32 = jnp.float32
BF16 = jnp.bfloat16

NORM_EPS = 1e-6
GRID_W = 64
ROPE_BASE = 10000.0
NEG_INF = -1e30
LOG2_E = 1.4426950408889634
LANE = 128
SUBLANE = 8
VMEM_LIMIT = 56 * 1024 * 1024

MLA_HEADS = 6
MLA_Q_RANK = 256
MLA_KV_RANK = 128
MLA_NOPE = 64
MLA_ROPE = 32
MLA_V = 64
SSM_HEADS = 6
SSM_HEAD_DIM = 64
SSM_D_INNER = SSM_HEADS * SSM_HEAD_DIM
SSM_GROUPS = 2
SSM_STATE = 128
SSM_CHUNK = 128
SSM_CONV_CH = SSM_D_INNER + 2 * SSM_GROUPS * SSM_STATE
SWA_HEADS = 4
SWA_KV_HEADS = 2
SWA_HEAD_DIM = 64
WINDOW = 128
BLOCK = 128
FFN_HIDDEN = 2816

IN_SPLITS = (MLA_Q_RANK, MLA_KV_RANK, MLA_ROPE, SSM_D_INNER, SSM_CONV_CH, 2 * SSM_HEADS,
             SWA_HEADS * SWA_HEAD_DIM, SWA_KV_HEADS * SWA_HEAD_DIM, SWA_KV_HEADS * SWA_HEAD_DIM)

MXU_COLS = 256
O_QA = 0
O_KVA = O_QA + MLA_Q_RANK
O_KR = O_KVA + MLA_KV_RANK
O_Z = O_KR + LANE
O_DT = O_Z + SSM_D_INNER
O_XBC = O_DT + LANE
O_SWK = O_XBC + SSM_CONV_CH
O_SWQ = O_SWK + LANE
O_SWV = O_SWQ + SWA_HEADS * LANE
O_SWVR = O_SWV + LANE
IN_WIDTH_P = O_SWVR + LANE
IN_GROUPS = (O_QA, O_Z, O_XBC, O_SWQ, IN_WIDTH_P)
assert all(g % MXU_COLS == 0 for g in IN_GROUPS)


def _cparams(sem):
    return pltpu.CompilerParams(dimension_semantics=sem, vmem_limit_bytes=VMEM_LIMIT)


def _rms(x, g):
    return x * lax.rsqrt(jnp.mean(x * x, axis=-1, keepdims=True) + NORM_EPS) * g


def _silu(x):
    return x * (1.0 / (1.0 + jnp.exp(-x)))


def _softplus(x):
    return jnp.maximum(x, 0.0) + jnp.log1p(jnp.exp(-jnp.abs(x)))


def _dot(a, b):
    return jnp.dot(a, b, preferred_element_type=F32)


def _dot_nt(a, b):
    return lax.dot_general(a, b, (((1,), (1,)), ((), ())), preferred_element_type=F32)


def _const_spec(shape):
    nd = len(shape)
    return pl.BlockSpec(shape, lambda *_: (0,) * nd)


def _mod_kernel(c_ref, w_ref, b_ref, o_ref):
    s = _silu(c_ref[...]).astype(BF16)
    o_ref[0] = _dot(s, w_ref[0].astype(BF16)) + b_ref[0]


def _modulation(cc, w_mod, b_mod):
    nl, d, n = w_mod.shape
    r = cc.shape[0]
    tn = 1536
    return pl.pallas_call(
        _mod_kernel,
        out_shape=jax.ShapeDtypeStruct((nl, r, n), F32),
        grid=(nl, n // tn),
        in_specs=[pl.BlockSpec((r, d), lambda l, j: (0, 0)),
                  pl.BlockSpec((1, d, tn), lambda l, j: (l, 0, j)),
                  pl.BlockSpec((1, 1, tn), lambda l, j: (l, 0, j))],
        out_specs=pl.BlockSpec((1, r, tn), lambda l, j: (l, 0, j)),
        compiler_params=_cparams(("arbitrary", "arbitrary")),
        name="modulation",
    )(cc, w_mod, b_mod.reshape(nl, 1, n))


def _rope(x, c, s, n, lane):
    first = (lane & (2 * n - 1)) < n
    p = jnp.where(first, pltpu.roll(x, LANE - n, 1), pltpu.roll(x, n, 1))
    return x * c + p * s


def _ones_lane(h):
    return (1 - h % 2) * MLA_V


def _in_proj_kernel(x_ref, xp_ref, xn_ref, mod_ref, g1_ref, win_ref, gq_ref, gkv_ref, wuq_ref, wukv_ref,
                    cw_ref, cb_ref, cm_ref, sm_ref, cs_ref, ss_ref,
                    q_ref, k_ref, v_ref, z_ref, xbc_ref, dt_ref, swq_ref, swk_ref, swv_ref, swvr_ref):
    i = pl.program_id(1)
    nt = pl.num_programs(1)
    tm = x_ref.shape[1]
    te = tm + 2 * SUBLANE
    mod = mod_ref[0]
    xe = jnp.concatenate([xp_ref[0], x_ref[0], xn_ref[0]], axis=0)
    he = (_rms(xe, g1_ref[...]) * (1.0 + mod[1:2]) + mod[0:1]).astype(BF16)
    hb = he[SUBLANE:SUBLANE + tm]

    conv_group = IN_GROUPS.index(O_XBC)
    groups = {}

    def project(gi):
        groups[gi] = _dot(he if gi == conv_group else hb, win_ref[:, IN_GROUPS[gi]:IN_GROUPS[gi + 1]])

    project(0)
    project(1)

    def seg(a, b, halo=False):
        gi = max(i for i, g in enumerate(IN_GROUPS[:-1]) if g <= a)
        g = groups[gi][:, a - IN_GROUPS[gi]:b - IN_GROUPS[gi]]
        return g if halo or gi != conv_group else g[SUBLANE:SUBLANE + tm]

    lane = lax.broadcasted_iota(jnp.int32, (tm, LANE), 1)
    cm, sm, cs, ss = cm_ref[...], sm_ref[...], cs_ref[...], ss_ref[...]

    qn = _rms(seg(O_QA, O_KVA), gq_ref[...]).astype(BF16)
    qf = _dot(qn, wuq_ref[...])
    q_scale = (MLA_NOPE + MLA_ROPE) ** -0.5 * LOG2_E
    for hh in range(MLA_HEADS):
        qh = _rope(qf[:, hh * LANE:(hh + 1) * LANE], cm, sm, MLA_ROPE // 4, lane)
        q_ref[0, hh] = (qh * q_scale).astype(BF16)
    kn = _rms(seg(O_KVA, O_KR), gkv_ref[...]).astype(BF16)
    kvf = _dot(kn, wukv_ref[...])
    kr = _rope(seg(O_KR, O_Z), cm, sm, MLA_ROPE // 4, lane)
    for hh in range(MLA_HEADS):
        k_ref[0, hh] = (kvf[:, hh * LANE:(hh + 1) * LANE] + kr).astype(BF16)
        vh = kvf[:, (MLA_HEADS + hh) * LANE:(MLA_HEADS + hh + 1) * LANE]
        v_ref[0, hh] = jnp.where(lane == _ones_lane(hh), 1.0, vh).astype(BF16)

    project(2)
    z_ref[0] = seg(O_Z, O_DT)
    dt_ref[0] = seg(O_DT, O_XBC)
    project(3)
    rid = lax.broadcasted_iota(jnp.int32, (te, 1), 0)
    inside = ((rid >= SUBLANE) | (i > 0)) & ((rid < tm + SUBLANE) | (i < nt - 1))
    xbc = jnp.where(inside, seg(O_XBC, O_SWK, halo=True), 0.0)
    cw = cw_ref[...]
    xm1 = pltpu.roll(xbc, 1, 0)[SUBLANE:SUBLANE + tm]
    xp1 = pltpu.roll(xbc, te - 1, 0)[SUBLANE:SUBLANE + tm]
    xbc_ref[0] = _silu(cw[0:1] * xm1 + cw[1:2] * xbc[SUBLANE:SUBLANE + tm] + cw[2:3] * xp1 + cb_ref[...])

    sw_scale = SWA_HEAD_DIM ** -0.5 * LOG2_E
    for hh in range(SWA_HEADS):
        qh = _rope(seg(O_SWQ + hh * LANE, O_SWQ + (hh + 1) * LANE), cs, ss, SWA_HEAD_DIM // 4, lane)
        swq_ref[0, hh] = (qh * sw_scale).astype(BF16)
    swk_ref[0] = _rope(seg(O_SWK, O_SWQ), cs, ss, SWA_HEAD_DIM // 4, lane).astype(BF16)
    swv_ref[0] = seg(O_SWV, O_SWVR).astype(BF16)
    swvr_ref[0] = seg(O_SWVR, IN_WIDTH_P).astype(BF16)


def _in_proj(x, mod, mod_batched, p, tables, tm):
    b, s, d = x.shape
    nt = s // tm
    mod_map = (lambda bi, i: (bi, 0, 0)) if mod_batched else (lambda bi, i: (0, 0, 0))
    tok = lambda w: pl.BlockSpec((1, tm, w), lambda bi, i: (bi, i, 0))
    heads = lambda nh: pl.BlockSpec((1, nh, tm, LANE), lambda bi, i: (bi, 0, i, 0))
    tab = pl.BlockSpec((tm, LANE), lambda bi, i: (i, 0))
    out_shape = (
        jax.ShapeDtypeStruct((b, MLA_HEADS, s, LANE), BF16),
        jax.ShapeDtypeStruct((b, MLA_HEADS, s, LANE), BF16),
        jax.ShapeDtypeStruct((b, MLA_HEADS, s, LANE), BF16),
        jax.ShapeDtypeStruct((b, s, SSM_D_INNER), F32),
        jax.ShapeDtypeStruct((b, s, SSM_CONV_CH), F32),
        jax.ShapeDtypeStruct((b, s, LANE), F32),
        jax.ShapeDtypeStruct((b, SWA_HEADS, s, LANE), BF16),
        jax.ShapeDtypeStruct((b, s, LANE), BF16),
        jax.ShapeDtypeStruct((b, s, LANE), BF16),
        jax.ShapeDtypeStruct((b, s, LANE), BF16),
    )
    out_specs = (heads(MLA_HEADS), heads(MLA_HEADS), heads(MLA_HEADS), tok(SSM_D_INNER), tok(SSM_CONV_CH),
                 tok(LANE), heads(SWA_HEADS), tok(LANE), tok(LANE), tok(LANE))
    r8 = tm // SUBLANE
    n8 = s // SUBLANE
    return pl.pallas_call(
        _in_proj_kernel,
        out_shape=out_shape,
        grid=(b, nt),
        in_specs=[tok(d),
                  pl.BlockSpec((1, SUBLANE, d), lambda bi, i: (bi, jnp.maximum(i * r8 - 1, 0), 0)),
                  pl.BlockSpec((1, SUBLANE, d), lambda bi, i: (bi, jnp.minimum((i + 1) * r8, n8 - 1), 0)),
                  pl.BlockSpec((1, 6, d), mod_map),
                  _const_spec((1, d)),
                  _const_spec(p['w_in'].shape),
                  _const_spec((1, MLA_Q_RANK)),
                  _const_spec((1, MLA_KV_RANK)),
                  _const_spec(p['w_uq'].shape),
                  _const_spec(p['w_ukv'].shape),
                  _const_spec((3, SSM_CONV_CH)), _const_spec((1, SSM_CONV_CH)),
                  tab, tab, tab, tab],
        out_specs=out_specs,
        compiler_params=_cparams(("parallel", "parallel")),
        name="in_proj",
    )(x, x, x, mod, p['norm1_g'], p['w_in'], p['gq'], p['gkv'], p['w_uq'], p['w_ukv'],
      p['conv_w'], p['conv_b'], *tables)


def _lane_fold(x, op):
    r = x[:, :LANE]
    for i in range(1, x.shape[1] // LANE):
        r = op(r, x[:, i * LANE:(i + 1) * LANE])
    return r


def _mla_finish(acc, h):
    half = lax.broadcasted_iota(jnp.int32, acc.shape, 1) // MLA_V
    l = acc[:, _ones_lane(h):_ones_lane(h) + 1]
    return jnp.where(half == h % 2, acc / l, 0.0)


def _mla_ctx_kernel(q_ref, kc_ref, vc_ref, o_ref):
    for hp in range(q_ref.shape[1] // 2):
        out = None
        for r in range(2):
            hh = 2 * hp + r
            s = _dot_nt(q_ref[0, hh], kc_ref[0, hh])
            p = jnp.exp2(s - jnp.max(s, axis=-1, keepdims=True))
            o = _mla_finish(_dot(p.astype(BF16), vc_ref[0, hh]), hh)
            out = o if out is None else out + o
        o_ref[0, :, hp * LANE:(hp + 1) * LANE] = out


def _mla_kernel(q_ref, kc_ref, vc_ref, k_ref, v_ref, o_ref, sc_ref, s_ref, mrun_ref, mfin_ref, acc_ref, *, tk):
    nh = q_ref.shape[1]
    c = kc_ref.shape[2]
    nk = k_ref.shape[2] // tk

    def scores_ctx(h, slot):
        s = _dot_nt(q_ref[0, h], kc_ref[0, h])
        sc_ref[slot] = s
        mrun_ref[...] = _lane_fold(s, jnp.maximum)

    def scores_tile(h, slot, j):
        off = pl.multiple_of(j * tk, tk)
        s = _dot_nt(q_ref[0, h], k_ref[0, h, pl.ds(off, tk), :])
        s_ref[slot, j] = s
        mrun_ref[...] = jnp.maximum(mrun_ref[...], _lane_fold(s, jnp.maximum))

    def finish_max(slot):
        m = jnp.max(mrun_ref[...], axis=-1, keepdims=True)
        mfin_ref[slot] = jnp.broadcast_to(m, mfin_ref.shape[1:])

    def probs(s, m):
        return jnp.concatenate([jnp.exp2(s[:, i * LANE:(i + 1) * LANE] - m) for i in range(s.shape[1] // LANE)], axis=1)

    def values_ctx(h, slot):
        p = probs(sc_ref[slot], mfin_ref[slot])
        acc_ref[...] = _dot(p.astype(BF16), vc_ref[0, h])

    def values_tile(h, slot, j):
        off = pl.multiple_of(j * tk, tk)
        p = probs(s_ref[slot, j], mfin_ref[slot])
        acc_ref[...] += _dot(p.astype(BF16), v_ref[0, h, pl.ds(off, tk), :])

    def finish_out(h):
        o = _mla_finish(acc_ref[...], h)
        blk = (h // 2) * LANE
        if h % 2 == 0:
            o_ref[0, :, blk:blk + LANE] = o
        else:
            o_ref[0, :, blk:blk + LANE] += o

    for h in range(nh + 1):
        slot, prev = h % 2, (h - 1) % 2
        if h < nh:
            scores_ctx(h, slot)
        if h > 0:
            values_ctx(h - 1, prev)

        def body(j, carry, h=h, slot=slot, prev=prev):
            if h < nh:
                scores_tile(h, slot, j)
            if h > 0:
                values_tile(h - 1, prev, j)
            return carry

        lax.fori_loop(0, nk, body, 0, unroll=True)
        if h < nh:
            finish_max(slot)
        if h > 0:
            finish_out(h - 1)


def _mla(q, kc, vc, k, v, tq, tk):
    b, nh, s, _ = q.shape
    c = kc.shape[2]
    allh = lambda n: pl.BlockSpec((1, nh, n, LANE), lambda bi, i: (bi, 0, 0, 0))
    q_spec = pl.BlockSpec((1, nh, tq, LANE), lambda bi, i: (bi, 0, i, 0))
    out_shape = jax.ShapeDtypeStruct((b, s, nh * MLA_V), F32)
    out_spec = pl.BlockSpec((1, tq, nh * MLA_V), lambda bi, i: (bi, i, 0))
    if k is None:
        return pl.pallas_call(
            _mla_ctx_kernel, out_shape=out_shape, grid=(b, s // tq),
            in_specs=[q_spec, allh(c), allh(c)], out_specs=out_spec,
            compiler_params=_cparams(("parallel", "parallel")), name="mla_context",
        )(q, kc, vc)
    sk = k.shape[2]
    return pl.pallas_call(
        functools.partial(_mla_kernel, tk=tk),
        out_shape=out_shape,
        grid=(b, s // tq),
        in_specs=[q_spec, allh(c), allh(c), allh(sk), allh(sk)],
        out_specs=out_spec,
        scratch_shapes=[pltpu.VMEM((2, tq, c), F32), pltpu.VMEM((2, sk // tk, tq, tk), F32), pltpu.VMEM((tq, LANE), F32),
                        pltpu.VMEM((2, tq, LANE), F32), pltpu.VMEM((tq, LANE), F32)],
        compiler_params=_cparams(("parallel", "arbitrary")),
        name="mla_latent",
    )(q, kc, vc, k, v)


def _swa_kernel(*refs, has_band, seq):
    if has_band:
        (sink_ref, q_ref, kc_ref, vc_ref, vcr_ref, kp_ref, k0_ref, kn_ref,
         vp_ref, v0_ref, vn_ref, vrp_ref, vr0_ref, vrn_ref, o_ref) = refs
    else:
        sink_ref, q_ref, kc_ref, vc_ref, vcr_ref, o_ref = refs
    tq = q_ref.shape[2]
    c = kc_ref.shape[1]
    i = pl.program_id(1)
    half = lax.broadcasted_iota(jnp.int32, (tq, LANE), 1) // SWA_HEAD_DIM
    if has_band:
        kall = jnp.concatenate([kc_ref[0], kp_ref[0], k0_ref[0], kn_ref[0]], axis=0)
        vall = jnp.concatenate([vc_ref[0], vp_ref[0], v0_ref[0], vn_ref[0]], axis=0)
        vrall = jnp.concatenate([vcr_ref[0], vrp_ref[0], vr0_ref[0], vrn_ref[0]], axis=0)
        nk = c + tq + 2 * WINDOW
        colk = lax.broadcasted_iota(jnp.int32, (tq, nk), 1)
        qpos = i * tq + lax.broadcasted_iota(jnp.int32, (tq, nk), 0)
        kpos = colk + (i * tq - WINDOW - c)
        valid = (colk < c) | ((jnp.abs(kpos - qpos) <= WINDOW) & (kpos >= 0) & (kpos < seq))
    else:
        kall, vall, vrall = kc_ref[0], vc_ref[0], vcr_ref[0]
    rep = SWA_HEADS // SWA_KV_HEADS
    heads = range(SWA_HEADS)
    s_all = [_dot_nt(q_ref[0, hh], kall) for hh in heads]
    p_all, l_all = [], []
    for hh in heads:
        sink = sink_ref[hh] * LOG2_E
        s = jnp.where(valid, s_all[hh], NEG_INF) if has_band else s_all[hh]
        m = jnp.maximum(jnp.max(s, axis=-1, keepdims=True), sink)
        p = jnp.exp2(s - m)
        l_all.append(jnp.sum(p, axis=-1, keepdims=True) + jnp.exp2(sink - m))
        p_all.append(p.astype(BF16))
    o_all = [_dot(p_all[hh], vall if hh // rep == hh % rep else vrall) for hh in heads]
    for g in range(SWA_KV_HEADS):
        og = None
        for r in range(rep):
            hh = g * rep + r
            o = jnp.where(half == r, o_all[hh] / l_all[hh], 0.0)
            og = o if og is None else og + o
        o_ref[0, :, g * LANE:(g + 1) * LANE] = og


def _swa(sink, q, kc, vc, vcr, k, v, vr, tq):
    b, nh, s, _ = q.shape
    c = kc.shape[1]
    nb = s // tq
    has_band = k is not None
    ctx_spec = pl.BlockSpec((1, c, LANE), lambda bi, i: (bi, 0, 0))
    in_specs = [pl.BlockSpec(memory_space=pltpu.SMEM),
                pl.BlockSpec((1, nh, tq, LANE), lambda bi, i: (bi, 0, i, 0)),
                ctx_spec, ctx_spec, ctx_spec]
    args = [sink, q, kc, vc, vcr]
    if has_band:
        wpt = tq // WINDOW
        nw = s // WINDOW
        prev = pl.BlockSpec((1, WINDOW, LANE), lambda bi, i: (bi, jnp.maximum(i * wpt - 1, 0), 0))
        cur = pl.BlockSpec((1, tq, LANE), lambda bi, i: (bi, i, 0))
        nxt = pl.BlockSpec((1, WINDOW, LANE), lambda bi, i: (bi, jnp.minimum((i + 1) * wpt, nw - 1), 0))
        in_specs += [prev, cur, nxt] * 3
        args += [k, k, k, v, v, v, vr, vr, vr]
    return pl.pallas_call(
        functools.partial(_swa_kernel, has_band=has_band, seq=s),
        out_shape=jax.ShapeDtypeStruct((b, s, nh * SWA_HEAD_DIM), F32),
        grid=(b, nb),
        in_specs=in_specs,
        out_specs=pl.BlockSpec((1, tq, nh * SWA_HEAD_DIM), lambda bi, i: (bi, i, 0)),
        compiler_params=_cparams(("parallel", "parallel")),
        name="swa_latent" if has_band else "swa_context",
    )(*args)


def _split_dot(t_bf16, v):
    hi = v.astype(BF16)
    r1 = v - hi.astype(F32)
    mid = r1.astype(BF16)
    lo = (r1 - mid.astype(F32)).astype(BF16)
    return _dot(t_bf16, hi) + _dot(t_bf16, mid) + _dot(t_bf16, lo)


def _split_lhs_dot(v, t_bf16, pieces):
    out, rest = None, v
    for i in range(pieces):
        part = rest.astype(BF16)
        if i + 1 < pieces:
            rest = rest - part.astype(F32)
        term = _dot(part, t_bf16)
        out = term if out is None else out + term
    return out


def _ssd_kernel(xf_ref, xb_ref, dtf_ref, dtb_ref, dtbias_ref, alog_ref, dskip_ref, ex_ref, h0_ref,
                yf_ref, yb_ref, hfin_ref, hst_ref):
    c = pl.program_id(1)
    nc = pl.num_programs(1)
    q = SSM_CHUNK
    gn = SSM_GROUPS * SSM_STATE

    @pl.when(c == 0)
    def _():
        hst_ref[...] = h0_ref[...]

    row = lax.broadcasted_iota(jnp.int32, (q, q), 0)
    col = lax.broadcasted_iota(jnp.int32, (q, q), 1)
    lane = lax.broadcasted_iota(jnp.int32, (1, LANE), 1)
    low_half = lane < SSM_HEAD_DIM
    srow = lax.broadcasted_iota(jnp.int32, (gn, SSM_D_INNER), 0) // SSM_STATE
    scol = lax.broadcasted_iota(jnp.int32, (gn, SSM_D_INNER), 1) // (SSM_D_INNER // SSM_GROUPS)
    gmask = srow == scol
    a_neg = -jnp.exp(alog_ref[...])

    dirs = ((xf_ref, dtf_ref, yf_ref), (xb_ref, dtb_ref, yb_ref))
    keeps = ((col <= row), (col >= row))
    chains = [dict(bi=bi, d=d) for bi in range(hst_ref.shape[0]) for d in range(2)]

    for ch in chains:
        bi, d = ch['bi'], ch['d']
        xc = dirs[d][0][bi]
        ch['xs'] = xc[:, :SSM_D_INNER]
        ch['bm'] = xc[:, SSM_D_INNER:SSM_D_INNER + gn]
        ch['cm'] = xc[:, SSM_D_INNER + gn:]
        ch['dt'] = _softplus(dirs[d][1][bi] + dtbias_ref[...])
        ch['acs'] = _split_dot(keeps[d].astype(BF16), ch['dt'] * a_neg)

    for ch in chains:
        d = ch['d']
        ch['acs_x'] = _split_lhs_dot(ch['acs'], ex_ref[d], 3)
        ch['dt_x'] = _split_lhs_dot(ch['dt'], ex_ref[d], 2)
        ch['acs_t'] = jnp.transpose(ch['acs'])
        ch['cb16'] = ch['cm'].astype(BF16)
        ch['bb16'] = ch['bm'].astype(BF16)
        ch['bt16'] = jnp.transpose(ch['bm']).astype(BF16)
        ch['cbg'] = [_dot_nt(ch['cb16'][:, g * SSM_STATE:(g + 1) * SSM_STATE],
                             ch['bb16'][:, g * SSM_STATE:(g + 1) * SSM_STATE]) for g in range(SSM_GROUPS)]

    for ch in chains:
        bi, d = ch['bi'], ch['d']
        acs_x = ch['acs_x']
        tot_x = acs_x[q - 1:q] if d == 0 else acs_x[0:1]
        ch['xdt'] = ch['xs'] * ch['dt_x']
        hs = hst_ref[bi, d]
        ch['y_off'] = _dot(ch['cb16'], hs.astype(BF16)) * jnp.exp(acs_x)
        states = _dot(ch['bt16'], (ch['xdt'] * jnp.exp(tot_x - acs_x)).astype(BF16))
        hst_ref[bi, d] = hs * jnp.exp(tot_x) + jnp.where(gmask, states, 0.0)

    for bb in range(SSM_HEADS // 2):
        for ch in chains:
            d = ch['d']
            xblk = ch['xdt'][:, bb * LANE:(bb + 1) * LANE]
            ablk = ch['acs_x'][:, bb * LANE:(bb + 1) * LANE]
            aswap = pltpu.roll(ablk, SSM_HEAD_DIM, 1)
            yb = None
            for r in range(2):
                hh = 2 * bb + r
                g = hh // (SSM_HEADS // SSM_GROUPS)
                j = d * SSM_HEADS + hh
                acol = jnp.where(low_half, ablk, aswap) if r == 0 else jnp.where(low_half, aswap, ablk)
                diff = acol - ch['acs_t'][j:j + 1, :]
                decay = jnp.exp(jnp.where(keeps[d], diff, NEG_INF))
                mh = (ch['cbg'][g] * decay).astype(BF16)
                xh = jnp.where(low_half if r == 0 else jnp.logical_not(low_half), xblk, 0.0).astype(BF16)
                t = _dot(mh, xh)
                yb = t if yb is None else yb + t
            ch.setdefault('y_blocks', []).append(yb)

    for ch in chains:
        bi, d = ch['bi'], ch['d']
        y = jnp.concatenate(ch['y_blocks'], axis=1) + ch['y_off']
        if d == 0:
            y = y + dskip_ref[...] * ch['xs']
        dirs[d][2][bi] = y

    @pl.when(c == nc - 1)
    def _():
        hfin_ref[...] = hst_ref[...]


def _head_expander():
    rows = jnp.arange(LANE)[None, :, None]
    heads = (jnp.arange(SSM_D_INNER) // SSM_HEAD_DIM)[None, None, :]
    dirs = jnp.arange(2)[:, None, None]
    return (rows == dirs * SSM_HEADS + heads).astype(BF16)


def _ssd(xbc, dt, h0, p, nbb):
    b, s, _ = xbc.shape
    q = SSM_CHUNK
    nc = s // q
    gn = SSM_GROUPS * SSM_STATE
    fw = lambda bi, c: (bi, c, 0)
    bw = lambda bi, c: (bi, nc - 1 - c, 0)
    ch = SSM_CONV_CH
    state_spec = pl.BlockSpec((nbb, 2, gn, SSM_D_INNER), lambda bi, c: (bi, 0, 0, 0))
    return pl.pallas_call(
        _ssd_kernel,
        out_shape=(jax.ShapeDtypeStruct((b, s, SSM_D_INNER), F32),
                   jax.ShapeDtypeStruct((b, s, SSM_D_INNER), F32),
                   jax.ShapeDtypeStruct((b, 2, gn, SSM_D_INNER), F32)),
        grid=(b // nbb, nc),
        in_specs=[pl.BlockSpec((nbb, q, ch), fw), pl.BlockSpec((nbb, q, ch), bw),
                  pl.BlockSpec((nbb, q, LANE), fw), pl.BlockSpec((nbb, q, LANE), bw),
                  _const_spec((1, LANE)), _const_spec((1, LANE)), _const_spec((1, SSM_D_INNER)),
                  _const_spec((2, LANE, SSM_D_INNER)), state_spec],
        out_specs=(pl.BlockSpec((nbb, q, SSM_D_INNER), fw), pl.BlockSpec((nbb, q, SSM_D_INNER), bw), state_spec),
        scratch_shapes=[pltpu.VMEM((nbb, 2, gn, SSM_D_INNER), F32)],
        compiler_params=_cparams(("parallel", "arbitrary")),
        name="ssd_scan",
    )(xbc, xbc, dt, dt, p['dt_bias'], p['a_log'], p['d_skip'], _head_expander(), h0)


N_HALO_INPUTS = 6


def _mix_ffn_kernel(*refs, final_norm, tf):
    halo = refs[:3 * N_HALO_INPUTS]
    rest = refs[3 * N_HALO_INPUTS:]
    if final_norm:
        mod_ref, gn_ref, wo_ref, g2_ref, wup_ref, cw_ref, cb_ref, wdn_ref, gfin_ref, o_ref, act_ref = rest
    else:
        mod_ref, gn_ref, wo_ref, g2_ref, wup_ref, cw_ref, cb_ref, wdn_ref, o_ref, act_ref = rest
    i = pl.program_id(1)
    nt = pl.num_programs(1)
    tm = halo[0].shape[1]
    te = tm + 2 * SUBLANE
    f = wdn_ref.shape[0]
    mod = mod_ref[0]
    xin, ya, yf, yb, z, yc = [jnp.concatenate([halo[3 * k + 1][0], halo[3 * k][0], halo[3 * k + 2][0]], axis=0)
                              for k in range(N_HALO_INPUTS)]

    ybn = _rms((yf + yb) * _silu(z), gn_ref[...])
    na = ya.shape[1]
    nb = na + ybn.shape[1]
    o = (_dot(ya.astype(BF16), wo_ref[0:na, :])
         + _dot(ybn.astype(BF16), wo_ref[na:nb, :])
         + _dot(yc.astype(BF16), wo_ref[nb:, :]))
    xe = xin + mod[2:3] * o
    x = xe[SUBLANE:SUBLANE + tm]

    he = (_rms(xe, g2_ref[...]) * (1.0 + mod[4:5]) + mod[3:4]).astype(BF16)
    hc = he[SUBLANE:SUBLANE + tm]
    rid = lax.broadcasted_iota(jnp.int32, (te, 1), 0)
    inside = ((rid >= SUBLANE) | (i > 0)) & ((rid < tm + SUBLANE) | (i < nt - 1))
    for j in range(f // tf):
        val = _dot(hc, wup_ref[:, j * tf:(j + 1) * tf])
        gate = jnp.where(inside, _dot(he, wup_ref[:, f + j * tf:f + (j + 1) * tf]), 0.0)
        cw = cw_ref[:, j * tf:(j + 1) * tf]
        gm1 = pltpu.roll(gate, 1, 0)[SUBLANE:SUBLANE + tm]
        gp1 = pltpu.roll(gate, te - 1, 0)[SUBLANE:SUBLANE + tm]
        gc = cw[0:1] * gm1 + cw[1:2] * gate[SUBLANE:SUBLANE + tm] + cw[2:3] * gp1 + cb_ref[:, j * tf:(j + 1) * tf]
        act_ref[:, j * tf:(j + 1) * tf] = (_silu(gc) * val).astype(BF16)
    out = x + mod[5:6] * _dot(act_ref[...], wdn_ref[...])
    if final_norm:
        out = _rms(out, gfin_ref[...])
    o_ref[0] = out


def _mix_ffn(x, ya, yf, yb, z, yc, mod, mod_batched, p, gfin, tm):
    b, s, d = x.shape
    f = p['w_down'].shape[0]
    nt = s // tm
    r8 = tm // SUBLANE
    n8 = s // SUBLANE
    mod_map = (lambda bi, i: (bi, 0, 0)) if mod_batched else (lambda bi, i: (0, 0, 0))
    final_norm = gfin is not None
    in_specs, args = [], []
    for a in (x, ya, yf, yb, z, yc):
        w = a.shape[2]
        in_specs += [pl.BlockSpec((1, tm, w), lambda bi, i: (bi, i, 0)),
                     pl.BlockSpec((1, SUBLANE, w), lambda bi, i: (bi, jnp.maximum(i * r8 - 1, 0), 0)),
                     pl.BlockSpec((1, SUBLANE, w), lambda bi, i: (bi, jnp.minimum((i + 1) * r8, n8 - 1), 0))]
        args += [a, a, a]
    in_specs += [pl.BlockSpec((1, 6, d), mod_map), _const_spec((1, SSM_D_INNER)), _const_spec(p['w_out'].shape),
                 _const_spec((1, d)), _const_spec(p['w_up'].shape), _const_spec((3, f)), _const_spec((1, f)),
                 _const_spec(p['w_down'].shape)]
    args += [mod, p['ssm_norm_g'], p['w_out'], p['norm2_g'], p['w_up'], p['ffn_conv_w'], p['ffn_conv_b'], p['w_down']]
    if final_norm:
        in_specs.append(_const_spec((1, d)))
        args.append(gfin)
    return pl.pallas_call(
        functools.partial(_mix_ffn_kernel, final_norm=final_norm, tf=256),
        out_shape=jax.ShapeDtypeStruct((b, s, d), F32),
        grid=(b, nt),
        in_specs=in_specs,
        out_specs=pl.BlockSpec((1, tm, d), lambda bi, i: (bi, i, 0)),
        scratch_shapes=[pltpu.VMEM((tm, f), BF16)],
        compiler_params=_cparams(("parallel", "parallel")),
        name="mix_ffn_final" if final_norm else "mix_ffn",
    )(*args)


def _rope_tables(s):
    t = jnp.arange(s)
    row = (t // GRID_W).astype(F32)[:, None]
    col = (t % GRID_W).astype(F32)[:, None]

    def cs(n):
        inv = jnp.power(ROPE_BASE, -jnp.arange(n, dtype=F32) / n)
        ar, ac = row * inv, col * inv
        cos = jnp.concatenate([jnp.cos(ar), jnp.cos(ar), jnp.cos(ac), jnp.cos(ac)], axis=1)
        sin = jnp.concatenate([-jnp.sin(ar), jnp.sin(ar), -jnp.sin(ac), jnp.sin(ac)], axis=1)
        return cos, sin

    c32, s32 = cs(MLA_ROPE // 4)
    c64, s64 = cs(SWA_HEAD_DIM // 4)
    ones = jnp.ones((s, LANE), F32)
    zeros = jnp.zeros((s, LANE), F32)
    cm = ones.at[:, MLA_NOPE:MLA_NOPE + MLA_ROPE].set(c32)
    sm = zeros.at[:, MLA_NOPE:MLA_NOPE + MLA_ROPE].set(s32)
    return cm, sm, jnp.concatenate([c64, c64], axis=1), jnp.concatenate([s64, s64], axis=1)


def _identity_tables(s):
    ones = jnp.ones((s, LANE), F32)
    zeros = jnp.zeros((s, LANE), F32)
    return ones, zeros, ones, zeros


def _layer_params(l, w):
    d = w['w_in'].shape[1]
    w_in = w['w_in'][l]
    idx = [0]
    for n in IN_SPLITS:
        idx.append(idx[-1] + n)
    qa, kva, kr, z, xbc, dtr, swq, swk, swv = [w_in[:, idx[i]:idx[i + 1]] for i in range(len(IN_SPLITS))]
    zc = lambda n: jnp.zeros((d, n), F32)
    hd = SWA_HEAD_DIM
    swq_blocks = []
    for hh in range(SWA_HEADS):
        qh = swq[:, hh * hd:(hh + 1) * hd]
        swq_blocks += [qh, zc(hd)] if hh < SWA_HEADS // SWA_KV_HEADS else [zc(hd), qh]
    w_in_p = jnp.concatenate(
        [qa, kva, zc(MLA_NOPE), kr, zc(LANE - MLA_NOPE - MLA_ROPE), z, dtr, zc(LANE - 2 * SSM_HEADS), xbc, swk]
        + swq_blocks + [swv, swv[:, hd:], swv[:, :hd]], axis=1).astype(BF16)

    w_uq = w['mla_w_uq'][l].reshape(MLA_Q_RANK, MLA_HEADS, MLA_NOPE + MLA_ROPE)
    w_uq_p = jnp.pad(w_uq, ((0, 0), (0, 0), (0, LANE - MLA_NOPE - MLA_ROPE))).reshape(MLA_Q_RANK, MLA_HEADS * LANE)
    w_ukv = w['mla_w_ukv'][l].reshape(MLA_KV_RANK, MLA_HEADS, MLA_NOPE + MLA_V)
    wk = jnp.pad(w_ukv[:, :, :MLA_NOPE], ((0, 0), (0, 0), (0, LANE - MLA_NOPE)))
    wv = w_ukv[:, :, MLA_NOPE:]
    zv = jnp.zeros_like(wv)
    even = (jnp.arange(MLA_HEADS) % 2 == 0)[None, :, None]
    wv_p = jnp.concatenate([jnp.where(even, wv, zv), jnp.where(even, zv, wv)], axis=2)
    w_ukv_p = jnp.concatenate([wk.reshape(MLA_KV_RANK, -1), wv_p.reshape(MLA_KV_RANK, -1)], axis=1)

    pad_row = lambda v: jnp.pad(v.reshape(1, -1), ((0, 0), (0, LANE - v.size)))
    return dict(
        norm1_g=w['norm1_g'][l][None], norm2_g=w['norm2_g'][l][None],
        w_in=w_in_p, gq=w['mla_q_norm_g'][l][None], gkv=w['mla_kv_norm_g'][l][None],
        w_uq=w_uq_p.astype(BF16), w_ukv=w_ukv_p.astype(BF16),
        conv_w=w['ssm_conv_w'][l], conv_b=w['ssm_conv_b'][l][None],
        dt_bias=pad_row(w['ssm_dt_bias'][l]), a_log=pad_row(w['ssm_a_log'][l]),
        d_skip=jnp.repeat(w['ssm_d'][l], SSM_HEAD_DIM)[None], ssm_norm_g=w['ssm_norm_g'][l][None],
        sink=w['swa_sink'][l], w_out=w['w_out'][l].astype(BF16),
        w_up=w['ffn_w_up'][l].astype(BF16), ffn_conv_w=w['ffn_conv_w'][l], ffn_conv_b=w['ffn_conv_b'][l][None],
        w_down=w['ffn_w_down'][l].astype(BF16),
    )


def kernel(x, c, ctx, c_ctx, w_mod, b_mod, norm1_g, norm2_g, w_in, mla_q_norm_g, mla_kv_norm_g, mla_w_uq,
           mla_w_ukv, ssm_conv_w, ssm_conv_b, ssm_dt_bias, ssm_a_log, ssm_d, ssm_norm_g, swa_sink, w_out,
           ffn_w_up, ffn_conv_w, ffn_conv_b, ffn_w_down, final_norm_g):
    w = dict(w_in=w_in, norm1_g=norm1_g, norm2_g=norm2_g, mla_q_norm_g=mla_q_norm_g, mla_kv_norm_g=mla_kv_norm_g,
             mla_w_uq=mla_w_uq, mla_w_ukv=mla_w_ukv, ssm_conv_w=ssm_conv_w, ssm_conv_b=ssm_conv_b,
             ssm_dt_bias=ssm_dt_bias, ssm_a_log=ssm_a_log, ssm_d=ssm_d, ssm_norm_g=ssm_norm_g, swa_sink=swa_sink,
             w_out=w_out, ffn_w_up=ffn_w_up, ffn_conv_w=ffn_conv_w, ffn_conv_b=ffn_conv_b, ffn_w_down=ffn_w_down)
    b, s, d = x.shape
    n_ctx = ctx.shape[1]
    depth = w_mod.shape[0]
    tm = min(512, s)
    tmc = min(512, n_ctx)

    rows = 2 * SUBLANE
    cc = jnp.concatenate([c, c_ctx[None], jnp.zeros((rows - b - 1, d), F32)], axis=0)
    mods = _modulation(cc, w_mod, b_mod).reshape(depth, rows, 6, d)

    lat_tables = _rope_tables(s)
    ctx_tables = _identity_tables(n_ctx)
    h_zero = jnp.zeros((b, 2, SSM_GROUPS * SSM_STATE, SSM_D_INNER), F32)

    xc = ctx
    for l in range(depth):
        p = _layer_params(l, w)
        mod = mods[l, :b]
        modc = mods[l, b:b + 1]
        last = l == depth - 1

        qc, kc, vc, zc, xbcc, dtc, swqc, swkc, swvc, swvrc = _in_proj(xc, modc, False, p, ctx_tables, tmc)
        q, k, v, z, xbc, dt, swq, swk, swv, swvr = _in_proj(x, mod, True, p, lat_tables, tm)

        ya = _mla(q, kc, vc, k, v, tq=min(256, s), tk=min(1024, s))
        nbb = 2 if b % 2 == 0 else 1
        yfc, ybc, h_ctx = _ssd(xbcc, dtc, h_zero, p, nbb)
        yf, yb, _ = _ssd(xbc, dt, h_ctx, p, nbb)
        yc = _swa(p['sink'], swq, swkc, swvc, swvrc, swk, swv, swvr, tq=min(256, s))

        x = _mix_ffn(x, ya, yf, yb, z, yc, mod, True, p, final_norm_g[None] if last else None, tm)

        if not last:
            yac = _mla(qc, kc, vc, None, None, tq=min(256, n_ctx), tk=None)
            ycc = _swa(p['sink'], swqc, swkc, swvc, swvrc, None, None, None, tq=min(256, n_ctx))
            xc = _mix_ffn(xc, yac, yfc, ybc, zc, ycc, modc, False, p, None, tmc)
    return x
```

```python
import functools

import jax
import jax.numpy as jnp
from jax import lax
from jax.experimental import pallas as pl
from jax.experimental.pallas import tpu as pltpu

F32 = jnp.float32
BF16 = jnp.bfloat16

NORM_EPS = 1e-6
GRID_W = 64
ROPE_BASE = 10000.0
NEG_INF = -1e30
LOG2_E = 1.4426950408889634
LANE = 128
SUBLANE = 8
VMEM_LIMIT = 56 * 1024 * 1024

MLA_HEADS = 6
MLA_Q_RANK = 256
MLA_KV_RANK = 128
MLA_NOPE = 64
MLA_ROPE = 32
MLA_V = 64
SSM_HEADS = 6
SSM_HEAD_DIM = 64
SSM_D_INNER = SSM_HEADS * SSM_HEAD_DIM
SSM_GROUPS = 2
SSM_STATE = 128
SSM_CHUNK = 128
SSM_CONV_CH = SSM_D_INNER + 2 * SSM_GROUPS * SSM_STATE
SWA_HEADS = 4
SWA_KV_HEADS = 2
SWA_HEAD_DIM = 64
WINDOW = 128
BLOCK = 128
FFN_HIDDEN = 2816

IN_SPLITS = (MLA_Q_RANK, MLA_KV_RANK, MLA_ROPE, SSM_D_INNER, SSM_CONV_CH, 2 * SSM_HEADS,
             SWA_HEADS * SWA_HEAD_DIM, SWA_KV_HEADS * SWA_HEAD_DIM, SWA_KV_HEADS * SWA_HEAD_DIM)

MXU_COLS = 256
O_QA = 0
O_KVA = O_QA + MLA_Q_RANK
O_KR = O_KVA + MLA_KV_RANK
O_Z = O_KR + LANE
O_DT = O_Z + SSM_D_INNER
O_XBC = O_DT + LANE
O_SWK = O_XBC + SSM_CONV_CH
O_SWQ = O_SWK + LANE
O_SWV = O_SWQ + SWA_HEADS * LANE
O_SWVR = O_SWV + LANE
IN_WIDTH_P = O_SWVR + LANE
IN_GROUPS = (O_QA, O_Z, O_XBC, O_SWQ, IN_WIDTH_P)
assert all(g % MXU_COLS == 0 for g in IN_GROUPS)


def _cparams(sem):
    return pltpu.CompilerParams(dimension_semantics=sem, vmem_limit_bytes=VMEM_LIMIT)


def _rms(x, g):
    return x * lax.rsqrt(jnp.mean(x * x, axis=-1, keepdims=True) + NORM_EPS) * g


def _silu(x):
    return x * (1.0 / (1.0 + jnp.exp(-x)))


def _softplus(x):
    return jnp.maximum(x, 0.0) + jnp.log1p(jnp.exp(-jnp.abs(x)))


def _dot(a, b):
    return jnp.dot(a, b, preferred_element_type=F32)


def _dot_nt(a, b):
    return lax.dot_general(a, b, (((1,), (1,)), ((), ())), preferred_element_type=F32)


def _const_spec(shape):
    nd = len(shape)
    return pl.BlockSpec(shape, lambda *_: (0,) * nd)


def _mod_kernel(c_ref, w_ref, b_ref, o_ref):
    s = _silu(c_ref[...]).astype(BF16)
    o_ref[0] = _dot(s, w_ref[0].astype(BF16)) + b_ref[0]


def _modulation(cc, w_mod, b_mod):
    nl, d, n = w_mod.shape
    r = cc.shape[0]
    tn = 1536
    return pl.pallas_call(
        _mod_kernel,
        out_shape=jax.ShapeDtypeStruct((nl, r, n), F32),
        grid=(nl, n // tn),
        in_specs=[pl.BlockSpec((r, d), lambda l, j: (0, 0)),
                  pl.BlockSpec((1, d, tn), lambda l, j: (l, 0, j)),
                  pl.BlockSpec((1, 1, tn), lambda l, j: (l, 0, j))],
        out_specs=pl.BlockSpec((1, r, tn), lambda l, j: (l, 0, j)),
        compiler_params=_cparams(("arbitrary", "arbitrary")),
        name="modulation",
    )(cc, w_mod, b_mod.reshape(nl, 1, n))


def _rope(x, c, s, n, lane):
    first = (lane & (2 * n - 1)) < n
    p = jnp.where(first, pltpu.roll(x, LANE - n, 1), pltpu.roll(x, n, 1))
    return x * c + p * s


def _ones_lane(h):
    return (1 - h % 2) * MLA_V


def _in_proj_kernel(x_ref, xp_ref, xn_ref, mod_ref, g1_ref, win_ref, gq_ref, gkv_ref, wuq_ref, wukv_ref,
                    cw_ref, cb_ref, cm_ref, sm_ref, cs_ref, ss_ref,
                    q_ref, k_ref, v_ref, z_ref, xbc_ref, dt_ref, swq_ref, swk_ref, swv_ref, swvr_ref):
    i = pl.program_id(1)
    nt = pl.num_programs(1)
    tm = x_ref.shape[1]
    te = tm + 2 * SUBLANE
    mod = mod_ref[0]
    xe = jnp.concatenate([xp_ref[0], x_ref[0], xn_ref[0]], axis=0)
    he = (_rms(xe, g1_ref[...]) * (1.0 + mod[1:2]) + mod[0:1]).astype(BF16)
    hb = he[SUBLANE:SUBLANE + tm]

    conv_group = IN_GROUPS.index(O_XBC)
    groups = {}

    def project(gi):
        groups[gi] = _dot(he if gi == conv_group else hb, win_ref[:, IN_GROUPS[gi]:IN_GROUPS[gi + 1]])

    project(0)
    project(1)

    def seg(a, b, halo=False):
        gi = max(i for i, g in enumerate(IN_GROUPS[:-1]) if g <= a)
        g = groups[gi][:, a - IN_GROUPS[gi]:b - IN_GROUPS[gi]]
        return g if halo or gi != conv_group else g[SUBLANE:SUBLANE + tm]

    lane = lax.broadcasted_iota(jnp.int32, (tm, LANE), 1)
    cm, sm, cs, ss = cm_ref[...], sm_ref[...], cs_ref[...], ss_ref[...]

    qn = _rms(seg(O_QA, O_KVA), gq_ref[...]).astype(BF16)
    qf = _dot(qn, wuq_ref[...])
    q_scale = (MLA_NOPE + MLA_ROPE) ** -0.5 * LOG2_E
    for hh in range(MLA_HEADS):
        qh = _rope(qf[:, hh * LANE:(hh + 1) * LANE], cm, sm, MLA_ROPE // 4, lane)
        q_ref[0, hh] = (qh * q_scale).astype(BF16)
    kn = _rms(seg(O_KVA, O_KR), gkv_ref[...]).astype(BF16)
    kvf = _dot(kn, wukv_ref[...])
    kr = _rope(seg(O_KR, O_Z), cm, sm, MLA_ROPE // 4, lane)
    for hh in range(MLA_HEADS):
        k_ref[0, hh] = (kvf[:, hh * LANE:(hh + 1) * LANE] + kr).astype(BF16)
        vh = kvf[:, (MLA_HEADS + hh) * LANE:(MLA_HEADS + hh + 1) * LANE]
        v_ref[0, hh] = jnp.where(lane == _ones_lane(hh), 1.0, vh).astype(BF16)

    project(2)
    z_ref[0] = seg(O_Z, O_DT)
    dt_ref[0] = seg(O_DT, O_XBC)
    project(3)
    rid = lax.broadcasted_iota(jnp.int32, (te, 1), 0)
    inside = ((rid >= SUBLANE) | (i > 0)) & ((rid < tm + SUBLANE) | (i < nt - 1))
    xbc = jnp.where(inside, seg(O_XBC, O_SWK, halo=True), 0.0)
    cw = cw_ref[...]
    xm1 = pltpu.roll(xbc, 1, 0)[SUBLANE:SUBLANE + tm]
    xp1 = pltpu.roll(xbc, te - 1, 0)[SUBLANE:SUBLANE + tm]
    xbc_ref[0] = _silu(cw[0:1] * xm1 + cw[1:2] * xbc[SUBLANE:SUBLANE + tm] + cw[2:3] * xp1 + cb_ref[...])

    sw_scale = SWA_HEAD_DIM ** -0.5 * LOG2_E
    for hh in range(SWA_HEADS):
        qh = _rope(seg(O_SWQ + hh * LANE, O_SWQ + (hh + 1) * LANE), cs, ss, SWA_HEAD_DIM // 4, lane)
        swq_ref[0, hh] = (qh * sw_scale).astype(BF16)
    swk_ref[0] = _rope(seg(O_SWK, O_SWQ), cs, ss, SWA_HEAD_DIM // 4, lane).astype(BF16)
    swv_ref[0] = seg(O_SWV, O_SWVR).astype(BF16)
    swvr_ref[0] = seg(O_SWVR, IN_WIDTH_P).astype(BF16)


def _in_proj(x, mod, mod_batched, p, tables, tm):
    b, s, d = x.shape
    nt = s // tm
    mod_map = (lambda bi, i: (bi, 0, 0)) if mod_batched else (lambda bi, i: (0, 0, 0))
    tok = lambda w: pl.BlockSpec((1, tm, w), lambda bi, i: (bi, i, 0))
    heads = lambda nh: pl.BlockSpec((1, nh, tm, LANE), lambda bi, i: (bi, 0, i, 0))
    tab = pl.BlockSpec((tm, LANE), lambda bi, i: (i, 0))
    out_shape = (
        jax.ShapeDtypeStruct((b, MLA_HEADS, s, LANE), BF16),
        jax.ShapeDtypeStruct((b, MLA_HEADS, s, LANE), BF16),
        jax.ShapeDtypeStruct((b, MLA_HEADS, s, LANE), BF16),
        jax.ShapeDtypeStruct((b, s, SSM_D_INNER), F32),
        jax.ShapeDtypeStruct((b, s, SSM_CONV_CH), F32),
        jax.ShapeDtypeStruct((b, s, LANE), F32),
        jax.ShapeDtypeStruct((b, SWA_HEADS, s, LANE), BF16),
        jax.ShapeDtypeStruct((b, s, LANE), BF16),
        jax.ShapeDtypeStruct((b, s, LANE), BF16),
        jax.ShapeDtypeStruct((b, s, LANE), BF16),
    )
    out_specs = (heads(MLA_HEADS), heads(MLA_HEADS), heads(MLA_HEADS), tok(SSM_D_INNER), tok(SSM_CONV_CH),
                 tok(LANE), heads(SWA_HEADS), tok(LANE), tok(LANE), tok(LANE))
    r8 = tm // SUBLANE
    n8 = s // SUBLANE
    return pl.pallas_call(
        _in_proj_kernel,
        out_shape=out_shape,
        grid=(b, nt),
        in_specs=[tok(d),
                  pl.BlockSpec((1, SUBLANE, d), lambda bi, i: (bi, jnp.maximum(i * r8 - 1, 0), 0)),
                  pl.BlockSpec((1, SUBLANE, d), lambda bi, i: (bi, jnp.minimum((i + 1) * r8, n8 - 1), 0)),
                  pl.BlockSpec((1, 6, d), mod_map),
                  _const_spec((1, d)),
                  _const_spec(p['w_in'].shape),
                  _const_spec((1, MLA_Q_RANK)),
                  _const_spec((1, MLA_KV_RANK)),
                  _const_spec(p['w_uq'].shape),
                  _const_spec(p['w_ukv'].shape),
                  _const_spec((3, SSM_CONV_CH)), _const_spec((1, SSM_CONV_CH)),
                  tab, tab, tab, tab],
        out_specs=out_specs,
        compiler_params=_cparams(("parallel", "parallel")),
        name="in_proj",
    )(x, x, x, mod, p['norm1_g'], p['w_in'], p['gq'], p['gkv'], p['w_uq'], p['w_ukv'],
      p['conv_w'], p['conv_b'], *tables)


def _lane_fold(x, op):
    r = x[:, :LANE]
    for i in range(1, x.shape[1] // LANE):
        r = op(r, x[:, i * LANE:(i + 1) * LANE])
    return r


def _mla_finish(acc, h):
    half = lax.broadcasted_iota(jnp.int32, acc.shape, 1) // MLA_V
    l = acc[:, _ones_lane(h):_ones_lane(h) + 1]
    return jnp.where(half == h % 2, acc / l, 0.0)


def _mla_ctx_kernel(q_ref, kc_ref, vc_ref, o_ref):
    for hp in range(q_ref.shape[1] // 2):
        out = None
        for r in range(2):
            hh = 2 * hp + r
            s = _dot_nt(q_ref[0, hh], kc_ref[0, hh])
            p = jnp.exp2(s - jnp.max(s, axis=-1, keepdims=True))
            o = _mla_finish(_dot(p.astype(BF16), vc_ref[0, hh]), hh)
            out = o if out is None else out + o
        o_ref[0, :, hp * LANE:(hp + 1) * LANE] = out


def _mla_kernel(q_ref, kc_ref, vc_ref, k_ref, v_ref, o_ref, sc_ref, s_ref, mrun_ref, mfin_ref, acc_ref, *, tk):
    nh = q_ref.shape[1]
    c = kc_ref.shape[2]
    nk = k_ref.shape[2] // tk

    def scores_ctx(h, slot):
        s = _dot_nt(q_ref[0, h], kc_ref[0, h])
        sc_ref[slot] = s
        mrun_ref[...] = _lane_fold(s, jnp.maximum)

    def scores_tile(h, slot, j):
        off = pl.multiple_of(j * tk, tk)
        s = _dot_nt(q_ref[0, h], k_ref[0, h, pl.ds(off, tk), :])
        s_ref[slot, j] = s
        mrun_ref[...] = jnp.maximum(mrun_ref[...], _lane_fold(s, jnp.maximum))

    def finish_max(slot):
        m = jnp.max(mrun_ref[...], axis=-1, keepdims=True)
        mfin_ref[slot] = jnp.broadcast_to(m, mfin_ref.shape[1:])

    def probs(s, m):
        return jnp.concatenate([jnp.exp2(s[:, i * LANE:(i + 1) * LANE] - m) for i in range(s.shape[1] // LANE)], axis=1)

    def values_ctx(h, slot):
        p = probs(sc_ref[slot], mfin_ref[slot])
        acc_ref[...] = _dot(p.astype(BF16), vc_ref[0, h])

    def values_tile(h, slot, j):
        off = pl.multiple_of(j * tk, tk)
        p = probs(s_ref[slot, j], mfin_ref[slot])
        acc_ref[...] += _dot(p.astype(BF16), v_ref[0, h, pl.ds(off, tk), :])

    def finish_out(h):
        o = _mla_finish(acc_ref[...], h)
        blk = (h // 2) * LANE
        if h % 2 == 0:
            o_ref[0, :, blk:blk + LANE] = o
        else:
            o_ref[0, :, blk:blk + LANE] += o

    for h in range(nh + 1):
        slot, prev = h % 2, (h - 1) % 2
        if h < nh:
            scores_ctx(h, slot)
        if h > 0:
            values_ctx(h - 1, prev)

        def body(j, carry, h=h, slot=slot, prev=prev):
            if h < nh:
                scores_tile(h, slot, j)
            if h > 0:
                values_tile(h - 1, prev, j)
            return carry

        lax.fori_loop(0, nk, body, 0, unroll=True)
        if h < nh:
            finish_max(slot)
        if h > 0:
            finish_out(h - 1)


def _mla(q, kc, vc, k, v, tq, tk):
    b, nh, s, _ = q.shape
    c = kc.shape[2]
    allh = lambda n: pl.BlockSpec((1, nh, n, LANE), lambda bi, i: (bi, 0, 0, 0))
    q_spec = pl.BlockSpec((1, nh, tq, LANE), lambda bi, i: (bi, 0, i, 0))
    out_shape = jax.ShapeDtypeStruct((b, s, nh * MLA_V), F32)
    out_spec = pl.BlockSpec((1, tq, nh * MLA_V), lambda bi, i: (bi, i, 0))
    if k is None:
        return pl.pallas_call(
            _mla_ctx_kernel, out_shape=out_shape, grid=(b, s // tq),
            in_specs=[q_spec, allh(c), allh(c)], out_specs=out_spec,
            compiler_params=_cparams(("parallel", "parallel")), name="mla_context",
        )(q, kc, vc)
    sk = k.shape[2]
    return pl.pallas_call(
        functools.partial(_mla_kernel, tk=tk),
        out_shape=out_shape,
        grid=(b, s // tq),
        in_specs=[q_spec, allh(c), allh(c), allh(sk), allh(sk)],
        out_specs=out_spec,
        scratch_shapes=[pltpu.VMEM((2, tq, c), F32), pltpu.VMEM((2, sk // tk, tq, tk), F32), pltpu.VMEM((tq, LANE), F32),
                        pltpu.VMEM((2, tq, LANE), F32), pltpu.VMEM((tq, LANE), F32)],
        compiler_params=_cparams(("parallel", "arbitrary")),
        name="mla_latent",
    )(q, kc, vc, k, v)


def _swa_kernel(*refs, has_band, seq):
    if has_band:
        (sink_ref, q_ref, kc_ref, vc_ref, vcr_ref, kp_ref, k0_ref, kn_ref,
         vp_ref, v0_ref, vn_ref, vrp_ref, vr0_ref, vrn_ref, o_ref) = refs
    else:
        sink_ref, q_ref, kc_ref, vc_ref, vcr_ref, o_ref = refs
    tq = q_ref.shape[2]
    c = kc_ref.shape[1]
    i = pl.program_id(1)
    half = lax.broadcasted_iota(jnp.int32, (tq, LANE), 1) // SWA_HEAD_DIM
    if has_band:
        kall = jnp.concatenate([kc_ref[0], kp_ref[0], k0_ref[0], kn_ref[0]], axis=0)
        vall = jnp.concatenate([vc_ref[0], vp_ref[0], v0_ref[0], vn_ref[0]], axis=0)
        vrall = jnp.concatenate([vcr_ref[0], vrp_ref[0], vr0_ref[0], vrn_ref[0]], axis=0)
        nk = c + tq + 2 * WINDOW
        colk = lax.broadcasted_iota(jnp.int32, (tq, nk), 1)
        qpos = i * tq + lax.broadcasted_iota(jnp.int32, (tq, nk), 0)
        kpos = colk + (i * tq - WINDOW - c)
        valid = (colk < c) | ((jnp.abs(kpos - qpos) <= WINDOW) & (kpos >= 0) & (kpos < seq))
    else:
        kall, vall, vrall = kc_ref[0], vc_ref[0], vcr_ref[0]
    rep = SWA_HEADS // SWA_KV_HEADS
    heads = range(SWA_HEADS)
    s_all = [_dot_nt(q_ref[0, hh], kall) for hh in heads]
    p_all, l_all = [], []
    for hh in heads:
        sink = sink_ref[hh] * LOG2_E
        s = jnp.where(valid, s_all[hh], NEG_INF) if has_band else s_all[hh]
        m = jnp.maximum(jnp.max(s, axis=-1, keepdims=True), sink)
        p = jnp.exp2(s - m)
        l_all.append(jnp.sum(p, axis=-1, keepdims=True) + jnp.exp2(sink - m))
        p_all.append(p.astype(BF16))
    o_all = [_dot(p_all[hh], vall if hh // rep == hh % rep else vrall) for hh in heads]
    for g in range(SWA_KV_HEADS):
        og = None
        for r in range(rep):
            hh = g * rep + r
            o = jnp.where(half == r, o_all[hh] / l_all[hh], 0.0)
            og = o if og is None else og + o
        o_ref[0, :, g * LANE:(g + 1) * LANE] = og


def _swa(sink, q, kc, vc, vcr, k, v, vr, tq):
    b, nh, s, _ = q.shape
    c = kc.shape[1]
    nb = s // tq
    has_band = k is not None
    ctx_spec = pl.BlockSpec((1, c, LANE), lambda bi, i: (bi, 0, 0))
    in_specs = [pl.BlockSpec(memory_space=pltpu.SMEM),
                pl.BlockSpec((1, nh, tq, LANE), lambda bi, i: (bi, 0, i, 0)),
                ctx_spec, ctx_spec, ctx_spec]
    args = [sink, q, kc, vc, vcr]
    if has_band:
        wpt = tq // WINDOW
        nw = s // WINDOW
        prev = pl.BlockSpec((1, WINDOW, LANE), lambda bi, i: (bi, jnp.maximum(i * wpt - 1, 0), 0))
        cur = pl.BlockSpec((1, tq, LANE), lambda bi, i: (bi, i, 0))
        nxt = pl.BlockSpec((1, WINDOW, LANE), lambda bi, i: (bi, jnp.minimum((i + 1) * wpt, nw - 1), 0))
        in_specs += [prev, cur, nxt] * 3
        args += [k, k, k, v, v, v, vr, vr, vr]
    return pl.pallas_call(
        functools.partial(_swa_kernel, has_band=has_band, seq=s),
        out_shape=jax.ShapeDtypeStruct((b, s, nh * SWA_HEAD_DIM), F32),
        grid=(b, nb),
        in_specs=in_specs,
        out_specs=pl.BlockSpec((1, tq, nh * SWA_HEAD_DIM), lambda bi, i: (bi, i, 0)),
        compiler_params=_cparams(("parallel", "parallel")),
        name="swa_latent" if has_band else "swa_context",
    )(*args)


def _split_dot(t_bf16, v):
    hi = v.astype(BF16)
    r1 = v - hi.astype(F32)
    mid = r1.astype(BF16)
    lo = (r1 - mid.astype(F32)).astype(BF16)
    return _dot(t_bf16, hi) + _dot(t_bf16, mid) + _dot(t_bf16, lo)


def _split_lhs_dot(v, t_bf16, pieces):
    out, rest = None, v
    for i in range(pieces):
        part = rest.astype(BF16)
        if i + 1 < pieces:
            rest = rest - part.astype(F32)
        term = _dot(part, t_bf16)
        out = term if out is None else out + term
    return out


def _ssd_kernel(xf_ref, xb_ref, dtf_ref, dtb_ref, dtbias_ref, alog_ref, dskip_ref, ex_ref, h0_ref,
                yf_ref, yb_ref, hfin_ref, hst_ref):
    c = pl.program_id(1)
    nc = pl.num_programs(1)
    q = SSM_CHUNK
    gn = SSM_GROUPS * SSM_STATE

    @pl.when(c == 0)
    def _():
        hst_ref[...] = h0_ref[...]

    row = lax.broadcasted_iota(jnp.int32, (q, q), 0)
    col = lax.broadcasted_iota(jnp.int32, (q, q), 1)
    lane = lax.broadcasted_iota(jnp.int32, (1, LANE), 1)
    low_half = lane < SSM_HEAD_DIM
    srow = lax.broadcasted_iota(jnp.int32, (gn, SSM_D_INNER), 0) // SSM_STATE
    scol = lax.broadcasted_iota(jnp.int32, (gn, SSM_D_INNER), 1) // (SSM_D_INNER // SSM_GROUPS)
    gmask = srow == scol
    a_neg = -jnp.exp(alog_ref[...])

    dirs = ((xf_ref, dtf_ref, yf_ref), (xb_ref, dtb_ref, yb_ref))
    keeps = ((col <= row), (col >= row))
    chains = [dict(bi=bi, d=d) for bi in range(hst_ref.shape[0]) for d in range(2)]

    for ch in chains:
        bi, d = ch['bi'], ch['d']
        xc = dirs[d][0][bi]
        ch['xs'] = xc[:, :SSM_D_INNER]
        ch['bm'] = xc[:, SSM_D_INNER:SSM_D_INNER + gn]
        ch['cm'] = xc[:, SSM_D_INNER + gn:]
        ch['dt'] = _softplus(dirs[d][1][bi] + dtbias_ref[...])
        ch['acs'] = _split_dot(keeps[d].astype(BF16), ch['dt'] * a_neg)

    for ch in chains:
        d = ch['d']
        ch['acs_x'] = _split_lhs_dot(ch['acs'], ex_ref[d], 3)
        ch['dt_x'] = _split_lhs_dot(ch['dt'], ex_ref[d], 2)
        ch['acs_t'] = jnp.transpose(ch['acs'])
        ch['cb16'] = ch['cm'].astype(BF16)
        ch['bb16'] = ch['bm'].astype(BF16)
        ch['bt16'] = jnp.transpose(ch['bm']).astype(BF16)
        ch['cbg'] = [_dot_nt(ch['cb16'][:, g * SSM_STATE:(g + 1) * SSM_STATE],
                             ch['bb16'][:, g * SSM_STATE:(g + 1) * SSM_STATE]) for g in range(SSM_GROUPS)]

    for ch in chains:
        bi, d = ch['bi'], ch['d']
        acs_x = ch['acs_x']
        tot_x = acs_x[q - 1:q] if d == 0 else acs_x[0:1]
        ch['xdt'] = ch['xs'] * ch['dt_x']
        hs = hst_ref[bi, d]
        ch['y_off'] = _dot(ch['cb16'], hs.astype(BF16)) * jnp.exp(acs_x)
        states = _dot(ch['bt16'], (ch['xdt'] * jnp.exp(tot_x - acs_x)).astype(BF16))
        hst_ref[bi, d] = hs * jnp.exp(tot_x) + jnp.where(gmask, states, 0.0)

    for bb in range(SSM_HEADS // 2):
        for ch in chains:
            d = ch['d']
            xblk = ch['xdt'][:, bb * LANE:(bb + 1) * LANE]
            ablk = ch['acs_x'][:, bb * LANE:(bb + 1) * LANE]
            aswap = pltpu.roll(ablk, SSM_HEAD_DIM, 1)
            yb = None
            for r in range(2):
                hh = 2 * bb + r
                g = hh // (SSM_HEADS // SSM_GROUPS)
                j = d * SSM_HEADS + hh
                acol = jnp.where(low_half, ablk, aswap) if r == 0 else jnp.where(low_half, aswap, ablk)
                diff = acol - ch['acs_t'][j:j + 1, :]
                decay = jnp.exp(jnp.where(keeps[d], diff, NEG_INF))
                mh = (ch['cbg'][g] * decay).astype(BF16)
                xh = jnp.where(low_half if r == 0 else jnp.logical_not(low_half), xblk, 0.0).astype(BF16)
                t = _dot(mh, xh)
                yb = t if yb is None else yb + t
            ch.setdefault('y_blocks', []).append(yb)

    for ch in chains:
        bi, d = ch['bi'], ch['d']
        y = jnp.concatenate(ch['y_blocks'], axis=1) + ch['y_off']
        if d == 0:
            y = y + dskip_ref[...] * ch['xs']
        dirs[d][2][bi] = y

    @pl.when(c == nc - 1)
    def _():
        hfin_ref[...] = hst_ref[...]


def _head_expander():
    rows = jnp.arange(LANE)[None, :, None]
    heads = (jnp.arange(SSM_D_INNER) // SSM_HEAD_DIM)[None, None, :]
    dirs = jnp.arange(2)[:, None, None]
    return (rows == dirs * SSM_HEADS + heads).astype(BF16)


def _ssd(xbc, dt, h0, p, nbb):
    b, s, _ = xbc.shape
    q = SSM_CHUNK
    nc = s // q
    gn = SSM_GROUPS * SSM_STATE
    fw = lambda bi, c: (bi, c, 0)
    bw = lambda bi, c: (bi, nc - 1 - c, 0)
    ch = SSM_CONV_CH
    state_spec = pl.BlockSpec((nbb, 2, gn, SSM_D_INNER), lambda bi, c: (bi, 0, 0, 0))
    return pl.pallas_call(
        _ssd_kernel,
        out_shape=(jax.ShapeDtypeStruct((b, s, SSM_D_INNER), F32),
                   jax.ShapeDtypeStruct((b, s, SSM_D_INNER), F32),
                   jax.ShapeDtypeStruct((b, 2, gn, SSM_D_INNER), F32)),
        grid=(b // nbb, nc),
        in_specs=[pl.BlockSpec((nbb, q, ch), fw), pl.BlockSpec((nbb, q, ch), bw),
                  pl.BlockSpec((nbb, q, LANE), fw), pl.BlockSpec((nbb, q, LANE), bw),
                  _const_spec((1, LANE)), _const_spec((1, LANE)), _const_spec((1, SSM_D_INNER)),
                  _const_spec((2, LANE, SSM_D_INNER)), state_spec],
        out_specs=(pl.BlockSpec((nbb, q, SSM_D_INNER), fw), pl.BlockSpec((nbb, q, SSM_D_INNER), bw), state_spec),
        scratch_shapes=[pltpu.VMEM((nbb, 2, gn, SSM_D_INNER), F32)],
        compiler_params=_cparams(("parallel", "arbitrary")),
        name="ssd_scan",
    )(xbc, xbc, dt, dt, p['dt_bias'], p['a_log'], p['d_skip'], _head_expander(), h0)


N_HALO_INPUTS = 6


def _mix_ffn_kernel(*refs, final_norm, tf):
    halo = refs[:3 * N_HALO_INPUTS]
    rest = refs[3 * N_HALO_INPUTS:]
    if final_norm:
        mod_ref, gn_ref, wo_ref, g2_ref, wup_ref, cw_ref, cb_ref, wdn_ref, gfin_ref, o_ref, act_ref = rest
    else:
        mod_ref, gn_ref, wo_ref, g2_ref, wup_ref, cw_ref, cb_ref, wdn_ref, o_ref, act_ref = rest
    i = pl.program_id(1)
    nt = pl.num_programs(1)
    tm = halo[0].shape[1]
    te = tm + 2 * SUBLANE
    f = wdn_ref.shape[0]
    mod = mod_ref[0]
    xin, ya, yf, yb, z, yc = [jnp.concatenate([halo[3 * k + 1][0], halo[3 * k][0], halo[3 * k + 2][0]], axis=0)
                              for k in range(N_HALO_INPUTS)]

    ybn = _rms((yf + yb) * _silu(z), gn_ref[...])
    o = _dot(jnp.concatenate([ya.astype(BF16), ybn.astype(BF16), yc.astype(BF16)], axis=1), wo_ref[...])
    xe = xin + mod[2:3] * o
    x = xe[SUBLANE:SUBLANE + tm]

    he = (_rms(xe, g2_ref[...]) * (1.0 + mod[4:5]) + mod[3:4]).astype(BF16)
    hc = he[SUBLANE:SUBLANE + tm]
    rid = lax.broadcasted_iota(jnp.int32, (te, 1), 0)
    inside = ((rid >= SUBLANE) | (i > 0)) & ((rid < tm + SUBLANE) | (i < nt - 1))
    for j in range(f // tf):
        val = _dot(hc, wup_ref[:, j * tf:(j + 1) * tf])
        gate = jnp.where(inside, _dot(he, wup_ref[:, f + j * tf:f + (j + 1) * tf]), 0.0)
        cw = cw_ref[:, j * tf:(j + 1) * tf]
        gm1 = pltpu.roll(gate, 1, 0)[SUBLANE:SUBLANE + tm]
        gp1 = pltpu.roll(gate, te - 1, 0)[SUBLANE:SUBLANE + tm]
        gc = cw[0:1] * gm1 + cw[1:2] * gate[SUBLANE:SUBLANE + tm] + cw[2:3] * gp1 + cb_ref[:, j * tf:(j + 1) * tf]
        act_ref[:, j * tf:(j + 1) * tf] = (_silu(gc) * val).astype(BF16)
    out = x + mod[5:6] * _dot(act_ref[...], wdn_ref[...])
    if final_norm:
        out = _rms(out, gfin_ref[...])
    o_ref[0] = out


def _mix_ffn(x, ya, yf, yb, z, yc, mod, mod_batched, p, gfin, tm):
    b, s, d = x.shape
    f = p['w_down'].shape[0]
    nt = s // tm
    r8 = tm // SUBLANE
    n8 = s // SUBLANE
    mod_map = (lambda bi, i: (bi, 0, 0)) if mod_batched else (lambda bi, i: (0, 0, 0))
    final_norm = gfin is not None
    in_specs, args = [], []
    for a in (x, ya, yf, yb, z, yc):
        w = a.shape[2]
        in_specs += [pl.BlockSpec((1, tm, w), lambda bi, i: (bi, i, 0)),
                     pl.BlockSpec((1, SUBLANE, w), lambda bi, i: (bi, jnp.maximum(i * r8 - 1, 0), 0)),
                     pl.BlockSpec((1, SUBLANE, w), lambda bi, i: (bi, jnp.minimum((i + 1) * r8, n8 - 1), 0))]
        args += [a, a, a]
    in_specs += [pl.BlockSpec((1, 6, d), mod_map), _const_spec((1, SSM_D_INNER)), _const_spec(p['w_out'].shape),
                 _const_spec((1, d)), _const_spec(p['w_up'].shape), _const_spec((3, f)), _const_spec((1, f)),
                 _const_spec(p['w_down'].shape)]
    args += [mod, p['ssm_norm_g'], p['w_out'], p['norm2_g'], p['w_up'], p['ffn_conv_w'], p['ffn_conv_b'], p['w_down']]
    if final_norm:
        in_specs.append(_const_spec((1, d)))
        args.append(gfin)
    return pl.pallas_call(
        functools.partial(_mix_ffn_kernel, final_norm=final_norm, tf=256),
        out_shape=jax.ShapeDtypeStruct((b, s, d), F32),
        grid=(b, nt),
        in_specs=in_specs,
        out_specs=pl.BlockSpec((1, tm, d), lambda bi, i: (bi, i, 0)),
        scratch_shapes=[pltpu.VMEM((tm, f), BF16)],
        compiler_params=_cparams(("parallel", "parallel")),
        name="mix_ffn_final" if final_norm else "mix_ffn",
    )(*args)


def _rope_tables(s):
    t = jnp.arange(s)
    row = (t // GRID_W).astype(F32)[:, None]
    col = (t % GRID_W).astype(F32)[:, None]

    def cs(n):
        inv = jnp.power(ROPE_BASE, -jnp.arange(n, dtype=F32) / n)
        ar, ac = row * inv, col * inv
        cos = jnp.concatenate([jnp.cos(ar), jnp.cos(ar), jnp.cos(ac), jnp.cos(ac)], axis=1)
        sin = jnp.concatenate([-jnp.sin(ar), jnp.sin(ar), -jnp.sin(ac), jnp.sin(ac)], axis=1)
        return cos, sin

    c32, s32 = cs(MLA_ROPE // 4)
    c64, s64 = cs(SWA_HEAD_DIM // 4)
    ones = jnp.ones((s, LANE), F32)
    zeros = jnp.zeros((s, LANE), F32)
    cm = ones.at[:, MLA_NOPE:MLA_NOPE + MLA_ROPE].set(c32)
    sm = zeros.at[:, MLA_NOPE:MLA_NOPE + MLA_ROPE].set(s32)
    return cm, sm, jnp.concatenate([c64, c64], axis=1), jnp.concatenate([s64, s64], axis=1)


def _identity_tables(s):
    ones = jnp.ones((s, LANE), F32)
    zeros = jnp.zeros((s, LANE), F32)
    return ones, zeros, ones, zeros


def _layer_params(l, w):
    d = w['w_in'].shape[1]
    w_in = w['w_in'][l]
    idx = [0]
    for n in IN_SPLITS:
        idx.append(idx[-1] + n)
    qa, kva, kr, z, xbc, dtr, swq, swk, swv = [w_in[:, idx[i]:idx[i + 1]] for i in range(len(IN_SPLITS))]
    zc = lambda n: jnp.zeros((d, n), F32)
    hd = SWA_HEAD_DIM
    swq_blocks = []
    for hh in range(SWA_HEADS):
        qh = swq[:, hh * hd:(hh + 1) * hd]
        swq_blocks += [qh, zc(hd)] if hh < SWA_HEADS // SWA_KV_HEADS else [zc(hd), qh]
    w_in_p = jnp.concatenate(
        [qa, kva, zc(MLA_NOPE), kr, zc(LANE - MLA_NOPE - MLA_ROPE), z, dtr, zc(LANE - 2 * SSM_HEADS), xbc, swk]
        + swq_blocks + [swv, swv[:, hd:], swv[:, :hd]], axis=1).astype(BF16)

    w_uq = w['mla_w_uq'][l].reshape(MLA_Q_RANK, MLA_HEADS, MLA_NOPE + MLA_ROPE)
    w_uq_p = jnp.pad(w_uq, ((0, 0), (0, 0), (0, LANE - MLA_NOPE - MLA_ROPE))).reshape(MLA_Q_RANK, MLA_HEADS * LANE)
    w_ukv = w['mla_w_ukv'][l].reshape(MLA_KV_RANK, MLA_HEADS, MLA_NOPE + MLA_V)
    wk = jnp.pad(w_ukv[:, :, :MLA_NOPE], ((0, 0), (0, 0), (0, LANE - MLA_NOPE)))
    wv = w_ukv[:, :, MLA_NOPE:]
    zv = jnp.zeros_like(wv)
    even = (jnp.arange(MLA_HEADS) % 2 == 0)[None, :, None]
    wv_p = jnp.concatenate([jnp.where(even, wv, zv), jnp.where(even, zv, wv)], axis=2)
    w_ukv_p = jnp.concatenate([wk.reshape(MLA_KV_RANK, -1), wv_p.reshape(MLA_KV_RANK, -1)], axis=1)

    pad_row = lambda v: jnp.pad(v.reshape(1, -1), ((0, 0), (0, LANE - v.size)))
    return dict(
        norm1_g=w['norm1_g'][l][None], norm2_g=w['norm2_g'][l][None],
        w_in=w_in_p, gq=w['mla_q_norm_g'][l][None], gkv=w['mla_kv_norm_g'][l][None],
        w_uq=w_uq_p.astype(BF16), w_ukv=w_ukv_p.astype(BF16),
        conv_w=w['ssm_conv_w'][l], conv_b=w['ssm_conv_b'][l][None],
        dt_bias=pad_row(w['ssm_dt_bias'][l]), a_log=pad_row(w['ssm_a_log'][l]),
        d_skip=jnp.repeat(w['ssm_d'][l], SSM_HEAD_DIM)[None], ssm_norm_g=w['ssm_norm_g'][l][None],
        sink=w['swa_sink'][l], w_out=w['w_out'][l].astype(BF16),
        w_up=w['ffn_w_up'][l].astype(BF16), ffn_conv_w=w['ffn_conv_w'][l], ffn_conv_b=w['ffn_conv_b'][l][None],
        w_down=w['ffn_w_down'][l].astype(BF16),
    )


def kernel(x, c, ctx, c_ctx, w_mod, b_mod, norm1_g, norm2_g, w_in, mla_q_norm_g, mla_kv_norm_g, mla_w_uq,
           mla_w_ukv, ssm_conv_w, ssm_conv_b, ssm_dt_bias, ssm_a_log, ssm_d, ssm_norm_g, swa_sink, w_out,
           ffn_w_up, ffn_conv_w, ffn_conv_b, ffn_w_down, final_norm_g):
    w = dict(w_in=w_in, norm1_g=norm1_g, norm2_g=norm2_g, mla_q_norm_g=mla_q_norm_g, mla_kv_norm_g=mla_kv_norm_g,
             mla_w_uq=mla_w_uq, mla_w_ukv=mla_w_ukv, ssm_conv_w=ssm_conv_w, ssm_conv_b=ssm_conv_b,
             ssm_dt_bias=ssm_dt_bias, ssm_a_log=ssm_a_log, ssm_d=ssm_d, ssm_norm_g=ssm_norm_g, swa_sink=swa_sink,
             w_out=w_out, ffn_w_up=ffn_w_up, ffn_conv_w=ffn_conv_w, ffn_conv_b=ffn_conv_b, ffn_w_down=ffn_w_down)
    b, s, d = x.shape
    n_ctx = ctx.shape[1]
    depth = w_mod.shape[0]
    tm = min(512, s)
    tmc = min(512, n_ctx)

    rows = 2 * SUBLANE
    cc = jnp.concatenate([c, c_ctx[None], jnp.zeros((rows - b - 1, d), F32)], axis=0)
    mods = _modulation(cc, w_mod, b_mod).reshape(depth, rows, 6, d)

    lat_tables = _rope_tables(s)
    ctx_tables = _identity_tables(n_ctx)
    h_zero = jnp.zeros((b, 2, SSM_GROUPS * SSM_STATE, SSM_D_INNER), F32)

    xc = ctx
    for l in range(depth):
        p = _layer_params(l, w)
        mod = mods[l, :b]
        modc = mods[l, b:b + 1]
        last = l == depth - 1

        qc, kc, vc, zc, xbcc, dtc, swqc, swkc, swvc, swvrc = _in_proj(xc, modc, False, p, ctx_tables, tmc)
        q, k, v, z, xbc, dt, swq, swk, swv, swvr = _in_proj(x, mod, True, p, lat_tables, tm)

        ya = _mla(q, kc, vc, k, v, tq=min(256, s), tk=min(1024, s))
        nbb = 2 if b % 2 == 0 else 1
        yfc, ybc, h_ctx = _ssd(xbcc, dtc, h_zero, p, nbb)
        yf, yb, _ = _ssd(xbc, dt, h_ctx, p, nbb)
        yc = _swa(p['sink'], swq, swkc, swvc, swvrc, swk, swv, swvr, tq=min(256, s))

        x = _mix_ffn(x, ya, yf, yb, z, yc, mod, True, p, final_norm_g[None] if last else None, tm)

        if not last:
            yac = _mla(qc, kc, vc, None, None, tq=min(256, n_ctx), tk=None)
            ycc = _swa(p['sink'], swqc, swkc, swvc, swvrc, None, None, None, tq=min(256, n_ctx))
            xc = _mix_ffn(xc, yac, yfc, ybc, zc, ycc, modc, False, p, None, tmc)
    return x
```

```python
import functools

import jax
import jax.numpy as jnp
from jax import lax
from jax.experimental import pallas as pl
from jax.experimental.pallas import tpu as pltpu

F32 = jnp.float32
BF16 = jnp.bfloat16

NORM_EPS = 1e-6
GRID_W = 64
ROPE_BASE = 10000.0
NEG_INF = -1e30
LOG2_E = 1.4426950408889634
LANE = 128
SUBLANE = 8
VMEM_LIMIT = 56 * 1024 * 1024

MLA_HEADS = 6
MLA_Q_RANK = 256
MLA_KV_RANK = 128
MLA_NOPE = 64
MLA_ROPE = 32
MLA_V = 64
SSM_HEADS = 6
SSM_HEAD_DIM = 64
SSM_D_INNER = SSM_HEADS * SSM_HEAD_DIM
SSM_GROUPS = 2
SSM_STATE = 128
SSM_CHUNK = 128
SSM_CONV_CH = SSM_D_INNER + 2 * SSM_GROUPS * SSM_STATE
SWA_HEADS = 4
SWA_KV_HEADS = 2
SWA_HEAD_DIM = 64
WINDOW = 128
BLOCK = 128
FFN_HIDDEN = 2816

IN_SPLITS = (MLA_Q_RANK, MLA_KV_RANK, MLA_ROPE, SSM_D_INNER, SSM_CONV_CH, 2 * SSM_HEADS,
             SWA_HEADS * SWA_HEAD_DIM, SWA_KV_HEADS * SWA_HEAD_DIM, SWA_KV_HEADS * SWA_HEAD_DIM)

MXU_COLS = 256
O_QA = 0
O_KVA = O_QA + MLA_Q_RANK
O_KR = O_KVA + MLA_KV_RANK
O_Z = O_KR + LANE
O_DT = O_Z + SSM_D_INNER
O_XBC = O_DT + LANE
O_SWK = O_XBC + SSM_CONV_CH
O_SWQ = O_SWK + LANE
O_SWV = O_SWQ + SWA_HEADS * LANE
O_SWVR = O_SWV + LANE
IN_WIDTH_P = O_SWVR + LANE
IN_GROUPS = (O_QA, O_Z, O_XBC, O_SWQ, IN_WIDTH_P)
assert all(g % MXU_COLS == 0 for g in IN_GROUPS)


def _cparams(sem):
    return pltpu.CompilerParams(dimension_semantics=sem, vmem_limit_bytes=VMEM_LIMIT)


def _rms(x, g):
    return x * lax.rsqrt(jnp.mean(x * x, axis=-1, keepdims=True) + NORM_EPS) * g


def _silu(x):
    return x * (1.0 / (1.0 + jnp.exp(-x)))


def _softplus(x):
    return jnp.maximum(x, 0.0) + jnp.log1p(jnp.exp(-jnp.abs(x)))


def _dot(a, b):
    return jnp.dot(a, b, preferred_element_type=F32)


def _dot_nt(a, b):
    return lax.dot_general(a, b, (((1,), (1,)), ((), ())), preferred_element_type=F32)


def _const_spec(shape):
    nd = len(shape)
    return pl.BlockSpec(shape, lambda *_: (0,) * nd)


def _mod_kernel(c_ref, w_ref, b_ref, o_ref):
    s = _silu(c_ref[...]).astype(BF16)
    o_ref[0] = _dot(s, w_ref[0].astype(BF16)) + b_ref[0]


def _modulation(cc, w_mod, b_mod):
    nl, d, n = w_mod.shape
    r = cc.shape[0]
    tn = 1536
    return pl.pallas_call(
        _mod_kernel,
        out_shape=jax.ShapeDtypeStruct((nl, r, n), F32),
        grid=(nl, n // tn),
        in_specs=[pl.BlockSpec((r, d), lambda l, j: (0, 0)),
                  pl.BlockSpec((1, d, tn), lambda l, j: (l, 0, j)),
                  pl.BlockSpec((1, 1, tn), lambda l, j: (l, 0, j))],
        out_specs=pl.BlockSpec((1, r, tn), lambda l, j: (l, 0, j)),
        compiler_params=_cparams(("arbitrary", "arbitrary")),
        name="modulation",
    )(cc, w_mod, b_mod.reshape(nl, 1, n))


def _rope(x, c, s, n, lane):
    first = (lane & (2 * n - 1)) < n
    p = jnp.where(first, pltpu.roll(x, LANE - n, 1), pltpu.roll(x, n, 1))
    return x * c + p * s


def _ones_lane(h):
    return (1 - h % 2) * MLA_V


def _in_proj_kernel(x_ref, xp_ref, xn_ref, mod_ref, g1_ref, win_ref, gq_ref, gkv_ref, wuq_ref, wukv_ref,
                    cw_ref, cb_ref, cm_ref, sm_ref, cs_ref, ss_ref,
                    q_ref, k_ref, v_ref, z_ref, xbc_ref, dt_ref, swq_ref, swk_ref, swv_ref, swvr_ref):
    i = pl.program_id(1)
    nt = pl.num_programs(1)
    tm = x_ref.shape[1]
    te = tm + 2 * SUBLANE
    mod = mod_ref[0]
    xe = jnp.concatenate([xp_ref[0], x_ref[0], xn_ref[0]], axis=0)
    conv_group = IN_GROUPS.index(O_XBC)
    n_groups = len(IN_GROUPS) - 1
    parts = [[] for _ in range(n_groups)]

    def norm_project(lo, hi):
        he = (_rms(xe[lo:hi], g1_ref[...]) * (1.0 + mod[1:2]) + mod[0:1]).astype(BF16)
        a, b = max(lo, SUBLANE), min(hi, SUBLANE + tm)
        for gi in range(n_groups):
            rows = he if gi == conv_group else he[a - lo:b - lo]
            parts[gi].append(_dot(rows, win_ref[:, IN_GROUPS[gi]:IN_GROUPS[gi + 1]]))

    norm_project(0, te // 2)
    norm_project(te // 2, te)
    groups = [jnp.concatenate(p, axis=0) for p in parts]

    def seg(a, b, halo=False):
        gi = max(i for i, g in enumerate(IN_GROUPS[:-1]) if g <= a)
        g = groups[gi][:, a - IN_GROUPS[gi]:b - IN_GROUPS[gi]]
        return g if halo or gi != conv_group else g[SUBLANE:SUBLANE + tm]

    lane = lax.broadcasted_iota(jnp.int32, (tm, LANE), 1)
    cm, sm, cs, ss = cm_ref[...], sm_ref[...], cs_ref[...], ss_ref[...]

    qn = _rms(seg(O_QA, O_KVA), gq_ref[...]).astype(BF16)
    qf = _dot(qn, wuq_ref[...])
    q_scale = (MLA_NOPE + MLA_ROPE) ** -0.5 * LOG2_E
    for hh in range(MLA_HEADS):
        qh = _rope(qf[:, hh * LANE:(hh + 1) * LANE], cm, sm, MLA_ROPE // 4, lane)
        q_ref[0, hh] = (qh * q_scale).astype(BF16)
    kn = _rms(seg(O_KVA, O_KR), gkv_ref[...]).astype(BF16)
    kvf = _dot(kn, wukv_ref[...])
    kr = _rope(seg(O_KR, O_Z), cm, sm, MLA_ROPE // 4, lane)
    for hh in range(MLA_HEADS):
        k_ref[0, hh] = (kvf[:, hh * LANE:(hh + 1) * LANE] + kr).astype(BF16)
        vh = kvf[:, (MLA_HEADS + hh) * LANE:(MLA_HEADS + hh + 1) * LANE]
        v_ref[0, hh] = jnp.where(lane == _ones_lane(hh), 1.0, vh).astype(BF16)

    z_ref[0] = seg(O_Z, O_DT)
    dt_ref[0] = seg(O_DT, O_XBC)
    rid = lax.broadcasted_iota(jnp.int32, (te, 1), 0)
    inside = ((rid >= SUBLANE) | (i > 0)) & ((rid < tm + SUBLANE) | (i < nt - 1))
    xbc = jnp.where(inside, seg(O_XBC, O_SWK, halo=True), 0.0)
    cw = cw_ref[...]
    xm1 = pltpu.roll(xbc, 1, 0)[SUBLANE:SUBLANE + tm]
    xp1 = pltpu.roll(xbc, te - 1, 0)[SUBLANE:SUBLANE + tm]
    xbc_ref[0] = _silu(cw[0:1] * xm1 + cw[1:2] * xbc[SUBLANE:SUBLANE + tm] + cw[2:3] * xp1 + cb_ref[...])

    sw_scale = SWA_HEAD_DIM ** -0.5 * LOG2_E
    for hh in range(SWA_HEADS):
        qh = _rope(seg(O_SWQ + hh * LANE, O_SWQ + (hh + 1) * LANE), cs, ss, SWA_HEAD_DIM // 4, lane)
        swq_ref[0, hh] = (qh * sw_scale).astype(BF16)
    swk_ref[0] = _rope(seg(O_SWK, O_SWQ), cs, ss, SWA_HEAD_DIM // 4, lane).astype(BF16)
    swv_ref[0] = seg(O_SWV, O_SWVR).astype(BF16)
    swvr_ref[0] = seg(O_SWVR, IN_WIDTH_P).astype(BF16)


def _in_proj(x, mod, mod_batched, p, tables, tm):
    b, s, d = x.shape
    nt = s // tm
    mod_map = (lambda bi, i: (bi, 0, 0)) if mod_batched else (lambda bi, i: (0, 0, 0))
    tok = lambda w: pl.BlockSpec((1, tm, w), lambda bi, i: (bi, i, 0))
    heads = lambda nh: pl.BlockSpec((1, nh, tm, LANE), lambda bi, i: (bi, 0, i, 0))
    tab = pl.BlockSpec((tm, LANE), lambda bi, i: (i, 0))
    out_shape = (
        jax.ShapeDtypeStruct((b, MLA_HEADS, s, LANE), BF16),
        jax.ShapeDtypeStruct((b, MLA_HEADS, s, LANE), BF16),
        jax.ShapeDtypeStruct((b, MLA_HEADS, s, LANE), BF16),
        jax.ShapeDtypeStruct((b, s, SSM_D_INNER), F32),
        jax.ShapeDtypeStruct((b, s, SSM_CONV_CH), F32),
        jax.ShapeDtypeStruct((b, s, LANE), F32),
        jax.ShapeDtypeStruct((b, SWA_HEADS, s, LANE), BF16),
        jax.ShapeDtypeStruct((b, s, LANE), BF16),
        jax.ShapeDtypeStruct((b, s, LANE), BF16),
        jax.ShapeDtypeStruct((b, s, LANE), BF16),
    )
    out_specs = (heads(MLA_HEADS), heads(MLA_HEADS), heads(MLA_HEADS), tok(SSM_D_INNER), tok(SSM_CONV_CH),
                 tok(LANE), heads(SWA_HEADS), tok(LANE), tok(LANE), tok(LANE))
    r8 = tm // SUBLANE
    n8 = s // SUBLANE
    return pl.pallas_call(
        _in_proj_kernel,
        out_shape=out_shape,
        grid=(b, nt),
        in_specs=[tok(d),
                  pl.BlockSpec((1, SUBLANE, d), lambda bi, i: (bi, jnp.maximum(i * r8 - 1, 0), 0)),
                  pl.BlockSpec((1, SUBLANE, d), lambda bi, i: (bi, jnp.minimum((i + 1) * r8, n8 - 1), 0)),
                  pl.BlockSpec((1, 6, d), mod_map),
                  _const_spec((1, d)),
                  _const_spec(p['w_in'].shape),
                  _const_spec((1, MLA_Q_RANK)),
                  _const_spec((1, MLA_KV_RANK)),
                  _const_spec(p['w_uq'].shape),
                  _const_spec(p['w_ukv'].shape),
                  _const_spec((3, SSM_CONV_CH)), _const_spec((1, SSM_CONV_CH)),
                  tab, tab, tab, tab],
        out_specs=out_specs,
        compiler_params=_cparams(("parallel", "parallel")),
        name="in_proj",
    )(x, x, x, mod, p['norm1_g'], p['w_in'], p['gq'], p['gkv'], p['w_uq'], p['w_ukv'],
      p['conv_w'], p['conv_b'], *tables)


def _lane_fold(x, op):
    r = x[:, :LANE]
    for i in range(1, x.shape[1] // LANE):
        r = op(r, x[:, i * LANE:(i + 1) * LANE])
    return r


def _mla_finish(acc, h):
    half = lax.broadcasted_iota(jnp.int32, acc.shape, 1) // MLA_V
    l = acc[:, _ones_lane(h):_ones_lane(h) + 1]
    return jnp.where(half == h % 2, acc / l, 0.0)


def _mla_ctx_kernel(q_ref, kc_ref, vc_ref, o_ref):
    for hp in range(q_ref.shape[1] // 2):
        out = None
        for r in range(2):
            hh = 2 * hp + r
            s = _dot_nt(q_ref[0, hh], kc_ref[0, hh])
            p = jnp.exp2(s - jnp.max(s, axis=-1, keepdims=True))
            o = _mla_finish(_dot(p.astype(BF16), vc_ref[0, hh]), hh)
            out = o if out is None else out + o
        o_ref[0, :, hp * LANE:(hp + 1) * LANE] = out


def _mla_kernel(q_ref, kc_ref, vc_ref, k_ref, v_ref, o_ref, sc_ref, s_ref, mrun_ref, mfin_ref, acc_ref, *, tk):
    nh = q_ref.shape[1]
    c = kc_ref.shape[2]
    nk = k_ref.shape[2] // tk

    def scores_ctx(h, slot):
        s = _dot_nt(q_ref[0, h], kc_ref[0, h])
        sc_ref[slot] = s
        mrun_ref[...] = _lane_fold(s, jnp.maximum)

    def scores_tile(h, slot, j):
        off = pl.multiple_of(j * tk, tk)
        s = _dot_nt(q_ref[0, h], k_ref[0, h, pl.ds(off, tk), :])
        s_ref[slot, j] = s
        mrun_ref[...] = jnp.maximum(mrun_ref[...], _lane_fold(s, jnp.maximum))

    def finish_max(slot):
        m = jnp.max(mrun_ref[...], axis=-1, keepdims=True)
        mfin_ref[slot] = jnp.broadcast_to(m, mfin_ref.shape[1:])

    def probs(s, m):
        return jnp.concatenate([jnp.exp2(s[:, i * LANE:(i + 1) * LANE] - m) for i in range(s.shape[1] // LANE)], axis=1)

    def values_ctx(h, slot):
        p = probs(sc_ref[slot], mfin_ref[slot])
        acc_ref[...] = _dot(p.astype(BF16), vc_ref[0, h])

    def values_tile(h, slot, j):
        off = pl.multiple_of(j * tk, tk)
        p = probs(s_ref[slot, j], mfin_ref[slot])
        acc_ref[...] += _dot(p.astype(BF16), v_ref[0, h, pl.ds(off, tk), :])

    def finish_out(h):
        o = _mla_finish(acc_ref[...], h)
        blk = (h // 2) * LANE
        if h % 2 == 0:
            o_ref[0, :, blk:blk + LANE] = o
        else:
            o_ref[0, :, blk:blk + LANE] += o

    for h in range(nh + 1):
        slot, prev = h % 2, (h - 1) % 2
        if h < nh:
            scores_ctx(h, slot)
        if h > 0:
            values_ctx(h - 1, prev)

        def body(j, carry, h=h, slot=slot, prev=prev):
            if h < nh:
                scores_tile(h, slot, j)
            if h > 0:
                values_tile(h - 1, prev, j)
            return carry

        lax.fori_loop(0, nk, body, 0, unroll=True)
        if h < nh:
            finish_max(slot)
        if h > 0:
            finish_out(h - 1)


def _mla(q, kc, vc, k, v, tq, tk):
    b, nh, s, _ = q.shape
    c = kc.shape[2]
    allh = lambda n: pl.BlockSpec((1, nh, n, LANE), lambda bi, i: (bi, 0, 0, 0), pipeline_mode=pl.Buffered(1))
    q_spec = pl.BlockSpec((1, nh, tq, LANE), lambda bi, i: (bi, 0, i, 0))
    out_shape = jax.ShapeDtypeStruct((b, s, nh * MLA_V), F32)
    out_spec = pl.BlockSpec((1, tq, nh * MLA_V), lambda bi, i: (bi, i, 0))
    if k is None:
        return pl.pallas_call(
            _mla_ctx_kernel, out_shape=out_shape, grid=(b, s // tq),
            in_specs=[q_spec, allh(c), allh(c)], out_specs=out_spec,
            compiler_params=_cparams(("parallel", "parallel")), name="mla_context",
        )(q, kc, vc)
    sk = k.shape[2]
    return pl.pallas_call(
        functools.partial(_mla_kernel, tk=tk),
        out_shape=out_shape,
        grid=(b, s // tq),
        in_specs=[q_spec, allh(c), allh(c), allh(sk), allh(sk)],
        out_specs=out_spec,
        scratch_shapes=[pltpu.VMEM((2, tq, c), F32), pltpu.VMEM((2, sk // tk, tq, tk), F32), pltpu.VMEM((tq, LANE), F32),
                        pltpu.VMEM((2, tq, LANE), F32), pltpu.VMEM((tq, LANE), F32)],
        compiler_params=_cparams(("parallel", "arbitrary")),
        name="mla_latent",
    )(q, kc, vc, k, v)


def _swa_kernel(*refs, has_band, seq):
    if has_band:
        (sink_ref, q_ref, kc_ref, vc_ref, vcr_ref, kp_ref, k0_ref, kn_ref,
         vp_ref, v0_ref, vn_ref, vrp_ref, vr0_ref, vrn_ref, o_ref) = refs
    else:
        sink_ref, q_ref, kc_ref, vc_ref, vcr_ref, o_ref = refs
    tq = q_ref.shape[2]
    c = kc_ref.shape[1]
    i = pl.program_id(1)
    half = lax.broadcasted_iota(jnp.int32, (tq, LANE), 1) // SWA_HEAD_DIM
    if has_band:
        kall = jnp.concatenate([kc_ref[0], kp_ref[0], k0_ref[0], kn_ref[0]], axis=0)
        vall = jnp.concatenate([vc_ref[0], vp_ref[0], v0_ref[0], vn_ref[0]], axis=0)
        vrall = jnp.concatenate([vcr_ref[0], vrp_ref[0], vr0_ref[0], vrn_ref[0]], axis=0)
        nk = c + tq + 2 * WINDOW
        colk = lax.broadcasted_iota(jnp.int32, (tq, nk), 1)
        qpos = i * tq + lax.broadcasted_iota(jnp.int32, (tq, nk), 0)
        kpos = colk + (i * tq - WINDOW - c)
        valid = (colk < c) | ((jnp.abs(kpos - qpos) <= WINDOW) & (kpos >= 0) & (kpos < seq))
    else:
        kall, vall, vrall = kc_ref[0], vc_ref[0], vcr_ref[0]
    rep = SWA_HEADS // SWA_KV_HEADS
    heads = range(SWA_HEADS)
    s_all = [_dot_nt(q_ref[0, hh], kall) for hh in heads]
    p_all, l_all = [], []
    for hh in heads:
        sink = sink_ref[hh] * LOG2_E
        s = jnp.where(valid, s_all[hh], NEG_INF) if has_band else s_all[hh]
        m = jnp.maximum(jnp.max(s, axis=-1, keepdims=True), sink)
        p = jnp.exp2(s - m)
        l_all.append(jnp.sum(p, axis=-1, keepdims=True) + jnp.exp2(sink - m))
        p_all.append(p.astype(BF16))
    o_all = [_dot(p_all[hh], vall if hh // rep == hh % rep else vrall) for hh in heads]
    for g in range(SWA_KV_HEADS):
        og = None
        for r in range(rep):
            hh = g * rep + r
            o = jnp.where(half == r, o_all[hh] / l_all[hh], 0.0)
            og = o if og is None else og + o
        o_ref[0, :, g * LANE:(g + 1) * LANE] = og


def _swa(sink, q, kc, vc, vcr, k, v, vr, tq):
    b, nh, s, _ = q.shape
    c = kc.shape[1]
    nb = s // tq
    has_band = k is not None
    ctx_spec = pl.BlockSpec((1, c, LANE), lambda bi, i: (bi, 0, 0))
    in_specs = [pl.BlockSpec(memory_space=pltpu.SMEM),
                pl.BlockSpec((1, nh, tq, LANE), lambda bi, i: (bi, 0, i, 0)),
                ctx_spec, ctx_spec, ctx_spec]
    args = [sink, q, kc, vc, vcr]
    if has_band:
        wpt = tq // WINDOW
        nw = s // WINDOW
        prev = pl.BlockSpec((1, WINDOW, LANE), lambda bi, i: (bi, jnp.maximum(i * wpt - 1, 0), 0))
        cur = pl.BlockSpec((1, tq, LANE), lambda bi, i: (bi, i, 0))
        nxt = pl.BlockSpec((1, WINDOW, LANE), lambda bi, i: (bi, jnp.minimum((i + 1) * wpt, nw - 1), 0))
        in_specs += [prev, cur, nxt] * 3
        args += [k, k, k, v, v, v, vr, vr, vr]
    return pl.pallas_call(
        functools.partial(_swa_kernel, has_band=has_band, seq=s),
        out_shape=jax.ShapeDtypeStruct((b, s, nh * SWA_HEAD_DIM), F32),
        grid=(b, nb),
        in_specs=in_specs,
        out_specs=pl.BlockSpec((1, tq, nh * SWA_HEAD_DIM), lambda bi, i: (bi, i, 0)),
        compiler_params=_cparams(("parallel", "parallel")),
        name="swa_latent" if has_band else "swa_context",
    )(*args)


def _split_dot(t_bf16, v):
    hi = v.astype(BF16)
    r1 = v - hi.astype(F32)
    mid = r1.astype(BF16)
    lo = (r1 - mid.astype(F32)).astype(BF16)
    return _dot(t_bf16, hi) + _dot(t_bf16, mid) + _dot(t_bf16, lo)


def _split_lhs_dot(v, t_bf16, pieces):
    out, rest = None, v
    for i in range(pieces):
        part = rest.astype(BF16)
        if i + 1 < pieces:
            rest = rest - part.astype(F32)
        term = _dot(part, t_bf16)
        out = term if out is None else out + term
    return out


def _ssd_kernel(xf_ref, xb_ref, dtf_ref, dtb_ref, dtbias_ref, alog_ref, dskip_ref, ex_ref, h0_ref,
                yf_ref, yb_ref, hfin_ref, hst_ref):
    c = pl.program_id(1)
    nc = pl.num_programs(1)
    q = SSM_CHUNK
    gn = SSM_GROUPS * SSM_STATE

    @pl.when(c == 0)
    def _():
        hst_ref[...] = h0_ref[...]

    row = lax.broadcasted_iota(jnp.int32, (q, q), 0)
    col = lax.broadcasted_iota(jnp.int32, (q, q), 1)
    lane = lax.broadcasted_iota(jnp.int32, (1, LANE), 1)
    low_half = lane < SSM_HEAD_DIM
    srow = lax.broadcasted_iota(jnp.int32, (gn, SSM_D_INNER), 0) // SSM_STATE
    scol = lax.broadcasted_iota(jnp.int32, (gn, SSM_D_INNER), 1) // (SSM_D_INNER // SSM_GROUPS)
    gmask = srow == scol
    a_neg = -jnp.exp(alog_ref[...])

    dirs = ((xf_ref, dtf_ref, yf_ref), (xb_ref, dtb_ref, yb_ref))
    keeps = ((col <= row), (col >= row))
    chains = [dict(bi=bi, d=d) for bi in range(hst_ref.shape[0]) for d in range(2)]

    for ch in chains:
        bi, d = ch['bi'], ch['d']
        xc = dirs[d][0][bi]
        ch['xs'] = xc[:, :SSM_D_INNER]
        ch['bm'] = xc[:, SSM_D_INNER:SSM_D_INNER + gn]
        ch['cm'] = xc[:, SSM_D_INNER + gn:]
        ch['dt'] = _softplus(dirs[d][1][bi] + dtbias_ref[...])
        ch['acs'] = _split_dot(keeps[d].astype(BF16), ch['dt'] * a_neg)

    for ch in chains:
        d = ch['d']
        ch['acs_x'] = _split_lhs_dot(ch['acs'], ex_ref[d], 3)
        ch['dt_x'] = _split_lhs_dot(ch['dt'], ex_ref[d], 2)
        ch['acs_t'] = jnp.transpose(ch['acs'])
        ch['cb16'] = ch['cm'].astype(BF16)
        ch['bb16'] = ch['bm'].astype(BF16)
        ch['bt16'] = jnp.transpose(ch['bm']).astype(BF16)
        ch['cbg'] = [_dot_nt(ch['cb16'][:, g * SSM_STATE:(g + 1) * SSM_STATE],
                             ch['bb16'][:, g * SSM_STATE:(g + 1) * SSM_STATE]) for g in range(SSM_GROUPS)]

    for ch in chains:
        bi, d = ch['bi'], ch['d']
        acs_x = ch['acs_x']
        tot_x = acs_x[q - 1:q] if d == 0 else acs_x[0:1]
        ch['xdt'] = ch['xs'] * ch['dt_x']
        hs = hst_ref[bi, d]
        ch['y_off'] = _dot(ch['cb16'], hs.astype(BF16)) * jnp.exp(acs_x)
        states = _dot(ch['bt16'], (ch['xdt'] * jnp.exp(tot_x - acs_x)).astype(BF16))
        hst_ref[bi, d] = hs * jnp.exp(tot_x) + jnp.where(gmask, states, 0.0)

    for bb in range(SSM_HEADS // 2):
        for ch in chains:
            d = ch['d']
            xblk = ch['xdt'][:, bb * LANE:(bb + 1) * LANE]
            ablk = ch['acs_x'][:, bb * LANE:(bb + 1) * LANE]
            aswap = pltpu.roll(ablk, SSM_HEAD_DIM, 1)
            yb = None
            for r in range(2):
                hh = 2 * bb + r
                g = hh // (SSM_HEADS // SSM_GROUPS)
                j = d * SSM_HEADS + hh
                acol = jnp.where(low_half, ablk, aswap) if r == 0 else jnp.where(low_half, aswap, ablk)
                diff = acol - ch['acs_t'][j:j + 1, :]
                decay = jnp.exp(jnp.where(keeps[d], diff, NEG_INF))
                mh = (ch['cbg'][g] * decay).astype(BF16)
                xh = jnp.where(low_half if r == 0 else jnp.logical_not(low_half), xblk, 0.0).astype(BF16)
                t = _dot(mh, xh)
                yb = t if yb is None else yb + t
            ch.setdefault('y_blocks', []).append(yb)

    for ch in chains:
        bi, d = ch['bi'], ch['d']
        y = jnp.concatenate(ch['y_blocks'], axis=1) + ch['y_off']
        if d == 0:
            y = y + dskip_ref[...] * ch['xs']
        dirs[d][2][bi] = y

    @pl.when(c == nc - 1)
    def _():
        hfin_ref[...] = hst_ref[...]


def _head_expander():
    rows = jnp.arange(LANE)[None, :, None]
    heads = (jnp.arange(SSM_D_INNER) // SSM_HEAD_DIM)[None, None, :]
    dirs = jnp.arange(2)[:, None, None]
    return (rows == dirs * SSM_HEADS + heads).astype(BF16)


def _ssd(xbc, dt, h0, p, nbb):
    b, s, _ = xbc.shape
    q = SSM_CHUNK
    nc = s // q
    gn = SSM_GROUPS * SSM_STATE
    fw = lambda bi, c: (bi, c, 0)
    bw = lambda bi, c: (bi, nc - 1 - c, 0)
    ch = SSM_CONV_CH
    state_spec = pl.BlockSpec((nbb, 2, gn, SSM_D_INNER), lambda bi, c: (bi, 0, 0, 0))
    return pl.pallas_call(
        _ssd_kernel,
        out_shape=(jax.ShapeDtypeStruct((b, s, SSM_D_INNER), F32),
                   jax.ShapeDtypeStruct((b, s, SSM_D_INNER), F32),
                   jax.ShapeDtypeStruct((b, 2, gn, SSM_D_INNER), F32)),
        grid=(b // nbb, nc),
        in_specs=[pl.BlockSpec((nbb, q, ch), fw), pl.BlockSpec((nbb, q, ch), bw),
                  pl.BlockSpec((nbb, q, LANE), fw), pl.BlockSpec((nbb, q, LANE), bw),
                  _const_spec((1, LANE)), _const_spec((1, LANE)), _const_spec((1, SSM_D_INNER)),
                  _const_spec((2, LANE, SSM_D_INNER)), state_spec],
        out_specs=(pl.BlockSpec((nbb, q, SSM_D_INNER), fw), pl.BlockSpec((nbb, q, SSM_D_INNER), bw), state_spec),
        scratch_shapes=[pltpu.VMEM((nbb, 2, gn, SSM_D_INNER), F32)],
        compiler_params=_cparams(("parallel", "arbitrary")),
        name="ssd_scan",
    )(xbc, xbc, dt, dt, p['dt_bias'], p['a_log'], p['d_skip'], _head_expander(), h0)


N_HALO_INPUTS = 6


def _mix_ffn_kernel(*refs, final_norm, tf):
    halo = refs[:3 * N_HALO_INPUTS]
    rest = refs[3 * N_HALO_INPUTS:]
    if final_norm:
        mod_ref, gn_ref, wo_ref, g2_ref, wup_ref, cw_ref, cb_ref, wdn_ref, gfin_ref, o_ref, act_ref = rest
    else:
        mod_ref, gn_ref, wo_ref, g2_ref, wup_ref, cw_ref, cb_ref, wdn_ref, o_ref, act_ref = rest
    i = pl.program_id(1)
    nt = pl.num_programs(1)
    tm = halo[0].shape[1]
    te = tm + 2 * SUBLANE
    f = wdn_ref.shape[0]
    mod = mod_ref[0]
    xin, ya, yf, yb, z, yc = [jnp.concatenate([halo[3 * k + 1][0], halo[3 * k][0], halo[3 * k + 2][0]], axis=0)
                              for k in range(N_HALO_INPUTS)]

    ybn = _rms((yf + yb) * _silu(z), gn_ref[...])
    o = _dot(jnp.concatenate([ya.astype(BF16), ybn.astype(BF16), yc.astype(BF16)], axis=1), wo_ref[...])
    xe = xin + mod[2:3] * o
    x = xe[SUBLANE:SUBLANE + tm]

    he = (_rms(xe, g2_ref[...]) * (1.0 + mod[4:5]) + mod[3:4]).astype(BF16)
    hc = he[SUBLANE:SUBLANE + tm]
    rid = lax.broadcasted_iota(jnp.int32, (te, 1), 0)
    inside = ((rid >= SUBLANE) | (i > 0)) & ((rid < tm + SUBLANE) | (i < nt - 1))
    for j in range(f // tf):
        val = _dot(hc, wup_ref[:, j * tf:(j + 1) * tf])
        gate = jnp.where(inside, _dot(he, wup_ref[:, f + j * tf:f + (j + 1) * tf]), 0.0)
        cw = cw_ref[:, j * tf:(j + 1) * tf]
        gm1 = pltpu.roll(gate, 1, 0)[SUBLANE:SUBLANE + tm]
        gp1 = pltpu.roll(gate, te - 1, 0)[SUBLANE:SUBLANE + tm]
        gc = cw[0:1] * gm1 + cw[1:2] * gate[SUBLANE:SUBLANE + tm] + cw[2:3] * gp1 + cb_ref[:, j * tf:(j + 1) * tf]
        act_ref[:, j * tf:(j + 1) * tf] = (_silu(gc) * val).astype(BF16)
    out = x + mod[5:6] * _dot(act_ref[...], wdn_ref[...])
    if final_norm:
        out = _rms(out, gfin_ref[...])
    o_ref[0] = out


def _mix_ffn(x, ya, yf, yb, z, yc, mod, mod_batched, p, gfin, tm):
    b, s, d = x.shape
    f = p['w_down'].shape[0]
    nt = s // tm
    r8 = tm // SUBLANE
    n8 = s // SUBLANE
    mod_map = (lambda bi, i: (bi, 0, 0)) if mod_batched else (lambda bi, i: (0, 0, 0))
    final_norm = gfin is not None
    in_specs, args = [], []
    for a in (x, ya, yf, yb, z, yc):
        w = a.shape[2]
        in_specs += [pl.BlockSpec((1, tm, w), lambda bi, i: (bi, i, 0)),
                     pl.BlockSpec((1, SUBLANE, w), lambda bi, i: (bi, jnp.maximum(i * r8 - 1, 0), 0)),
                     pl.BlockSpec((1, SUBLANE, w), lambda bi, i: (bi, jnp.minimum((i + 1) * r8, n8 - 1), 0))]
        args += [a, a, a]
    in_specs += [pl.BlockSpec((1, 6, d), mod_map), _const_spec((1, SSM_D_INNER)), _const_spec(p['w_out'].shape),
                 _const_spec((1, d)), _const_spec(p['w_up'].shape), _const_spec((3, f)), _const_spec((1, f)),
                 _const_spec(p['w_down'].shape)]
    args += [mod, p['ssm_norm_g'], p['w_out'], p['norm2_g'], p['w_up'], p['ffn_conv_w'], p['ffn_conv_b'], p['w_down']]
    if final_norm:
        in_specs.append(_const_spec((1, d)))
        args.append(gfin)
    return pl.pallas_call(
        functools.partial(_mix_ffn_kernel, final_norm=final_norm, tf=256),
        out_shape=jax.ShapeDtypeStruct((b, s, d), F32),
        grid=(b, nt),
        in_specs=in_specs,
        out_specs=pl.BlockSpec((1, tm, d), lambda bi, i: (bi, i, 0)),
        scratch_shapes=[pltpu.VMEM((tm, f), BF16)],
        compiler_params=_cparams(("parallel", "parallel")),
        name="mix_ffn_final" if final_norm else "mix_ffn",
    )(*args)


def _rope_tables(s):
    t = jnp.arange(s)
    row = (t // GRID_W).astype(F32)[:, None]
    col = (t % GRID_W).astype(F32)[:, None]

    def cs(n):
        inv = jnp.power(ROPE_BASE, -jnp.arange(n, dtype=F32) / n)
        ar, ac = row * inv, col * inv
        cos = jnp.concatenate([jnp.cos(ar), jnp.cos(ar), jnp.cos(ac), jnp.cos(ac)], axis=1)
        sin = jnp.concatenate([-jnp.sin(ar), jnp.sin(ar), -jnp.sin(ac), jnp.sin(ac)], axis=1)
        return cos, sin

    c32, s32 = cs(MLA_ROPE // 4)
    c64, s64 = cs(SWA_HEAD_DIM // 4)
    ones = jnp.ones((s, LANE), F32)
    zeros = jnp.zeros((s, LANE), F32)
    cm = ones.at[:, MLA_NOPE:MLA_NOPE + MLA_ROPE].set(c32)
    sm = zeros.at[:, MLA_NOPE:MLA_NOPE + MLA_ROPE].set(s32)
    return cm, sm, jnp.concatenate([c64, c64], axis=1), jnp.concatenate([s64, s64], axis=1)


def _identity_tables(s):
    ones = jnp.ones((s, LANE), F32)
    zeros = jnp.zeros((s, LANE), F32)
    return ones, zeros, ones, zeros


def _layer_params(l, w):
    d = w['w_in'].shape[1]
    w_in = w['w_in'][l]
    idx = [0]
    for n in IN_SPLITS:
        idx.append(idx[-1] + n)
    qa, kva, kr, z, xbc, dtr, swq, swk, swv = [w_in[:, idx[i]:idx[i + 1]] for i in range(len(IN_SPLITS))]
    zc = lambda n: jnp.zeros((d, n), F32)
    hd = SWA_HEAD_DIM
    swq_blocks = []
    for hh in range(SWA_HEADS):
        qh = swq[:, hh * hd:(hh + 1) * hd]
        swq_blocks += [qh, zc(hd)] if hh < SWA_HEADS // SWA_KV_HEADS else [zc(hd), qh]
    w_in_p = jnp.concatenate(
        [qa, kva, zc(MLA_NOPE), kr, zc(LANE - MLA_NOPE - MLA_ROPE), z, dtr, zc(LANE - 2 * SSM_HEADS), xbc, swk]
        + swq_blocks + [swv, swv[:, hd:], swv[:, :hd]], axis=1).astype(BF16)

    w_uq = w['mla_w_uq'][l].reshape(MLA_Q_RANK, MLA_HEADS, MLA_NOPE + MLA_ROPE)
    w_uq_p = jnp.pad(w_uq, ((0, 0), (0, 0), (0, LANE - MLA_NOPE - MLA_ROPE))).reshape(MLA_Q_RANK, MLA_HEADS * LANE)
    w_ukv = w['mla_w_ukv'][l].reshape(MLA_KV_RANK, MLA_HEADS, MLA_NOPE + MLA_V)
    wk = jnp.pad(w_ukv[:, :, :MLA_NOPE], ((0, 0), (0, 0), (0, LANE - MLA_NOPE)))
    wv = w_ukv[:, :, MLA_NOPE:]
    zv = jnp.zeros_like(wv)
    even = (jnp.arange(MLA_HEADS) % 2 == 0)[None, :, None]
    wv_p = jnp.concatenate([jnp.where(even, wv, zv), jnp.where(even, zv, wv)], axis=2)
    w_ukv_p = jnp.concatenate([wk.reshape(MLA_KV_RANK, -1), wv_p.reshape(MLA_KV_RANK, -1)], axis=1)

    pad_row = lambda v: jnp.pad(v.reshape(1, -1), ((0, 0), (0, LANE - v.size)))
    return dict(
        norm1_g=w['norm1_g'][l][None], norm2_g=w['norm2_g'][l][None],
        w_in=w_in_p, gq=w['mla_q_norm_g'][l][None], gkv=w['mla_kv_norm_g'][l][None],
        w_uq=w_uq_p.astype(BF16), w_ukv=w_ukv_p.astype(BF16),
        conv_w=w['ssm_conv_w'][l], conv_b=w['ssm_conv_b'][l][None],
        dt_bias=pad_row(w['ssm_dt_bias'][l]), a_log=pad_row(w['ssm_a_log'][l]),
        d_skip=jnp.repeat(w['ssm_d'][l], SSM_HEAD_DIM)[None], ssm_norm_g=w['ssm_norm_g'][l][None],
        sink=w['swa_sink'][l], w_out=w['w_out'][l].astype(BF16),
        w_up=w['ffn_w_up'][l].astype(BF16), ffn_conv_w=w['ffn_conv_w'][l], ffn_conv_b=w['ffn_conv_b'][l][None],
        w_down=w['ffn_w_down'][l].astype(BF16),
    )


def kernel(x, c, ctx, c_ctx, w_mod, b_mod, norm1_g, norm2_g, w_in, mla_q_norm_g, mla_kv_norm_g, mla_w_uq,
           mla_w_ukv, ssm_conv_w, ssm_conv_b, ssm_dt_bias, ssm_a_log, ssm_d, ssm_norm_g, swa_sink, w_out,
           ffn_w_up, ffn_conv_w, ffn_conv_b, ffn_w_down, final_norm_g):
    w = dict(w_in=w_in, norm1_g=norm1_g, norm2_g=norm2_g, mla_q_norm_g=mla_q_norm_g, mla_kv_norm_g=mla_kv_norm_g,
             mla_w_uq=mla_w_uq, mla_w_ukv=mla_w_ukv, ssm_conv_w=ssm_conv_w, ssm_conv_b=ssm_conv_b,
             ssm_dt_bias=ssm_dt_bias, ssm_a_log=ssm_a_log, ssm_d=ssm_d, ssm_norm_g=ssm_norm_g, swa_sink=swa_sink,
             w_out=w_out, ffn_w_up=ffn_w_up, ffn_conv_w=ffn_conv_w, ffn_conv_b=ffn_conv_b, ffn_w_down=ffn_w_down)
    b, s, d = x.shape
    n_ctx = ctx.shape[1]
    depth = w_mod.shape[0]
    tm = min(512, s)
    tmc = min(512, n_ctx)

    rows = 2 * SUBLANE
    cc = jnp.concatenate([c, c_ctx[None], jnp.zeros((rows - b - 1, d), F32)], axis=0)
    mods = _modulation(cc, w_mod, b_mod).reshape(depth, rows, 6, d)

    lat_tables = _rope_tables(s)
    ctx_tables = _identity_tables(n_ctx)
    h_zero = jnp.zeros((b, 2, SSM_GROUPS * SSM_STATE, SSM_D_INNER), F32)

    xc = ctx
    for l in range(depth):
        p = _layer_params(l, w)
        mod = mods[l, :b]
        modc = mods[l, b:b + 1]
        last = l == depth - 1

        qc, kc, vc, zc, xbcc, dtc, swqc, swkc, swvc, swvrc = _in_proj(xc, modc, False, p, ctx_tables, tmc)
        q, k, v, z, xbc, dt, swq, swk, swv, swvr = _in_proj(x, mod, True, p, lat_tables, tm)

        ya = _mla(q, kc, vc, k, v, tq=min(512, s), tk=min(1024, s))
        nbb = 2 if b % 2 == 0 else 1
        yfc, ybc, h_ctx = _ssd(xbcc, dtc, h_zero, p, nbb)
        yf, yb, _ = _ssd(xbc, dt, h_ctx, p, nbb)
        yc = _swa(p['sink'], swq, swkc, swvc, swvrc, swk, swv, swvr, tq=min(256, s))

        x = _mix_ffn(x, ya, yf, yb, z, yc, mod, True, p, final_norm_g[None] if last else None, tm)

        if not last:
            yac = _mla(qc, kc, vc, None, None, tq=min(256, n_ctx), tk=None)
            ycc = _swa(p['sink'], swqc, swkc, swvc, swvrc, None, None, None, tq=min(256, n_ctx))
            xc = _mix_ffn(xc, yac, yfc, ybc, zc, ycc, modc, False, p, None, tmc)
    return x
```
